```python
import jax, jax.numpy as jnp
from jax import lax
import numpy as np

D_MODEL = 2048
BATCH = 4
SEQ = 4096
DEPTH = 1

GRID_W = 64
MEM_LEN = 256
Q_BLOCK = 128
ROPE_THETA = 10000.0
EPS = 1e-6
MLA_HEADS = 8
MLA_Q_RANK = 512
MLA_KV_RANK = 512
MLA_NOPE = 128
MLA_ROPE = 64
MLA_V = 128
GQA_HEADS = 8
GQA_KV_HEADS = 2
GQA_HEAD_DIM = 128
X_HEADS = 4
X_HEAD_DIM = 128
D_FF = 256 * ((8 * D_MODEL // 3 + 255) // 256)
CONV_W = 3

MIX_A_WIDTH = MLA_HEADS * MLA_V
MIX_B_WIDTH = GQA_HEADS * GQA_HEAD_DIM
IN_SIZES = (MLA_Q_RANK, MLA_KV_RANK, MLA_ROPE,
            GQA_HEADS * GQA_HEAD_DIM, GQA_KV_HEADS * GQA_HEAD_DIM, GQA_KV_HEADS * GQA_HEAD_DIM)
N_IN = sum(IN_SIZES)
IN_SPLITS = [int(v) for v in np.cumsum(IN_SIZES)[:-1]]

kernel_name = 'hybrid_mla_gqa2d_gated_encoder_block'


def rms_norm(x, g):
    xf = x.astype(jnp.float32)
    y = xf * lax.rsqrt(jnp.mean(xf * xf, axis=-1, keepdims=True) + EPS)
    return (y * g.astype(jnp.float32)).astype(x.dtype)


def rope(x, pos):
    d = x.shape[-1]
    half = d // 2
    inv = jnp.power(ROPE_THETA, -2.0 * jnp.arange(half, dtype=jnp.float32) / d)
    ang = pos[:, None] * inv[None, :]
    cos = jnp.cos(ang)[None, :, None, :]
    sin = jnp.sin(ang)[None, :, None, :]
    xf = x.astype(jnp.float32)
    x1, x2 = xf[..., :half], xf[..., half:]
    return jnp.concatenate([x1 * cos - x2 * sin, x1 * sin + x2 * cos], axis=-1).astype(x.dtype)


def rope_2d(x, rows, cols):
    half = x.shape[-1] // 2
    return jnp.concatenate([rope(x[..., :half], rows), rope(x[..., half:], cols)], axis=-1)


def blocked_attention(q, k, v, scale):
    B, L, H, Dk = q.shape
    G = k.shape[2]
    R = H // G
    nb = L // Q_BLOCK
    qb = q.reshape(B, nb, Q_BLOCK, G, R, Dk).transpose(1, 0, 2, 3, 4, 5)

    def one_block(q_blk):
        s = jnp.einsum('bqgrd,bkgd->bgrqk', q_blk, k).astype(jnp.float32) * scale
        p = jax.nn.softmax(s, axis=-1).astype(v.dtype)
        return jnp.einsum('bgrqk,bkgd->bqgrd', p, v)

    out = lax.map(one_block, qb)
    return out.transpose(1, 0, 2, 3, 4, 5).reshape(B, L, H, v.shape[-1])


def mla_branch(c_q, c_kv, k_r, q_norm, w_uq, kv_norm, w_ukv, tpos):
    B, L, _ = c_q.shape
    q = (rms_norm(c_q, q_norm) @ w_uq).reshape(B, L, MLA_HEADS, MLA_NOPE + MLA_ROPE)
    q = jnp.concatenate([q[..., :MLA_NOPE], rope(q[..., MLA_NOPE:], tpos)], axis=-1)
    kv = (rms_norm(c_kv, kv_norm) @ w_ukv).reshape(B, L, MLA_HEADS, MLA_NOPE + MLA_V)
    k_nope, v = kv[..., :MLA_NOPE], kv[..., MLA_NOPE:]
    k_rope = rope(k_r[:, :, None, :], tpos)
    k = jnp.concatenate([k_nope, jnp.broadcast_to(k_rope, (B, L, MLA_HEADS, MLA_ROPE))], axis=-1)
    o = blocked_attention(q, k, v, (MLA_NOPE + MLA_ROPE) ** -0.5)
    return o.reshape(B, L, MIX_A_WIDTH)


def gqa_branch(qB, kB, vB, q_gain, k_gain, rows, cols):
    B, L, _ = qB.shape
    q = rms_norm(qB.reshape(B, L, GQA_HEADS, GQA_HEAD_DIM), q_gain)
    k = rms_norm(kB.reshape(B, L, GQA_KV_HEADS, GQA_HEAD_DIM), k_gain)
    v = vB.reshape(B, L, GQA_KV_HEADS, GQA_HEAD_DIM)
    q = rope_2d(q, rows, cols)
    k = rope_2d(k, rows, cols)
    o = blocked_attention(q, k, v, GQA_HEAD_DIM ** -0.5)
    return o.reshape(B, L, MIX_B_WIDTH)


def memory_cross_attention(hc, mem_n, w_xq, w_xkv, w_xo):
    B, L, _ = hc.shape
    M = mem_n.shape[1]
    q = (hc @ w_xq).reshape(B, L, X_HEADS, X_HEAD_DIM)
    kv = (mem_n @ w_xkv).reshape(B, M, X_HEADS, 2 * X_HEAD_DIM)
    k, v = kv[..., :X_HEAD_DIM], kv[..., X_HEAD_DIM:]
    o = blocked_attention(q, k, v, X_HEAD_DIM ** -0.5)
    return o.reshape(B, L, X_HEADS * X_HEAD_DIM) @ w_xo


def conv_glu_ffn(h, w_up, conv_w, conv_b, w_down):
    u = h @ w_up
    C = u.shape[-1]
    u = lax.conv_general_dilated(
        u, conv_w[:, None, :].astype(u.dtype), window_strides=(1,),
        padding=[(CONV_W // 2, CONV_W // 2)],
        dimension_numbers=('NWC', 'WIO', 'NWC'), feature_group_count=C) + conv_b
    a, b = u[..., :D_FF], u[..., D_FF:]
    return (jax.nn.silu(a) * b) @ w_down


def setup_inputs(seed: int = 0) -> dict:
    key = jax.random.key(seed)
    ks = jax.random.split(key, 32)
    f32 = jnp.float32

    def w(k, shape, fan_in):
        return jax.random.normal(k, shape, f32) * (fan_in ** -0.5)

    def gain(k, shape):
        return 1.0 + 0.1 * jax.random.normal(k, shape, f32)

    Dl = DEPTH
    D = D_MODEL
    return {
        'x': jax.random.normal(ks[0], (BATCH, SEQ, D), f32),
        'mem': jax.random.normal(ks[1], (BATCH, MEM_LEN, D), f32),
        'norm_mix': gain(ks[2], (Dl, D)),
        'w_in': w(ks[3], (Dl, D, N_IN), D),
        'mla_q_norm': gain(ks[4], (Dl, MLA_Q_RANK)),
        'w_uq': w(ks[5], (Dl, MLA_Q_RANK, MLA_HEADS * (MLA_NOPE + MLA_ROPE)), MLA_Q_RANK),
        'mla_kv_norm': gain(ks[6], (Dl, MLA_KV_RANK)),
        'w_ukv': w(ks[7], (Dl, MLA_KV_RANK, MLA_HEADS * (MLA_NOPE + MLA_V)), MLA_KV_RANK),
        'gqa_q_norm': gain(ks[8], (Dl, GQA_HEAD_DIM)),
        'gqa_k_norm': gain(ks[9], (Dl, GQA_HEAD_DIM)),
        'w_o_mla': w(ks[10], (Dl, MIX_A_WIDTH, D), MIX_A_WIDTH),
        'w_o_gqa': w(ks[11], (Dl, MIX_B_WIDTH, D), MIX_B_WIDTH),
        'w_gate': w(ks[12], (Dl, D, 2 * D), D),
        'b_gate': 0.1 * jax.random.normal(ks[13], (Dl, 2 * D), f32),
        'w_out': w(ks[14], (Dl, D, D), D),
        'norm_cross': gain(ks[15], (Dl, D)),
        'norm_mem': gain(ks[16], (Dl, D)),
        'w_xq': w(ks[17], (Dl, D, X_HEADS * X_HEAD_DIM), D),
        'w_xkv': w(ks[18], (Dl, D, 2 * X_HEADS * X_HEAD_DIM), D),
        'w_xo': w(ks[19], (Dl, X_HEADS * X_HEAD_DIM, D), X_HEADS * X_HEAD_DIM),
        'norm_ffn': gain(ks[20], (Dl, D)),
        'w_up': w(ks[21], (Dl, D, 2 * D_FF), D),
        'conv_w': w(ks[22], (Dl, CONV_W, 2 * D_FF), CONV_W),
        'conv_b': 0.02 * jax.random.normal(ks[23], (Dl, 2 * D_FF), f32),
        'w_down': w(ks[24], (Dl, D_FF, D), D_FF),
        'norm_final': gain(ks[25], (D,)),
    }


def reference(x, mem, norm_mix, w_in, mla_q_norm, w_uq, mla_kv_norm, w_ukv,
              gqa_q_norm, gqa_k_norm, w_o_mla, w_o_gqa, w_gate, b_gate, w_out,
              norm_cross, norm_mem, w_xq, w_xkv, w_xo,
              norm_ffn, w_up, conv_w, conv_b, w_down, norm_final):
    L = x.shape[1]
    ROWS = L // GRID_W
    tpos = jnp.arange(L, dtype=jnp.float32)
    rows = jnp.repeat(jnp.arange(ROWS, dtype=jnp.float32), GRID_W)
    cols = jnp.tile(jnp.arange(GRID_W, dtype=jnp.float32), ROWS)

    for l in range(DEPTH):
        h = rms_norm(x, norm_mix[l])
        c_q, c_kv, k_r, qB, kB, vB = jnp.split(h @ w_in[l], IN_SPLITS, axis=-1)
        yA = mla_branch(c_q, c_kv, k_r, mla_q_norm[l], w_uq[l], mla_kv_norm[l], w_ukv[l], tpos)
        yB = gqa_branch(qB, kB, vB, gqa_q_norm[l], gqa_k_norm[l], rows, cols)
        g = jax.nn.sigmoid((h @ w_gate[l] + b_gate[l]).astype(jnp.float32)).astype(x.dtype)
        gA, gB = g[..., :D_MODEL], g[..., D_MODEL:]
        m = gA * (yA @ w_o_mla[l]) + gB * (yB @ w_o_gqa[l])
        x = x + m @ w_out[l]
        hc = rms_norm(x, norm_cross[l])
        mem_n = rms_norm(mem, norm_mem[l])
        x = x + memory_cross_attention(hc, mem_n, w_xq[l], w_xkv[l], w_xo[l])
        hf = rms_norm(x, norm_ffn[l])
        x = x + conv_glu_ffn(hf, w_up[l], conv_w[l], conv_b[l], w_down[l])

    return rms_norm(x, norm_final)
```

```python
import functools

import jax
import jax.numpy as jnp
import numpy as np
from jax import lax
from jax.experimental import pallas as pl
from jax.experimental.pallas import tpu as pltpu

F32 = jnp.float32
BF16 = jnp.bfloat16

EPS = 1e-6
ROPE_THETA = 10000.0
GRID_W = 64
MLA_HEADS = 8
MLA_Q_RANK = 512
MLA_KV_RANK = 512
MLA_NOPE = 128
MLA_ROPE = 64
MLA_V = 128
GQA_HEADS = 8
GQA_KV_HEADS = 2
GQA_HEAD_DIM = 128
X_HEADS = 4
X_HEAD_DIM = 128
CONV_W = 3

LANES = 128
BF16_ROWS = 16
VMEM_LIMIT = 56 * 1024 * 1024
NEG_BIG = -1e30


def _params(semantics):
    return pltpu.CompilerParams(dimension_semantics=semantics, vmem_limit_bytes=VMEM_LIMIT)


def _resident(shape):
    zeros = (0,) * len(shape)
    return pl.BlockSpec(shape, lambda *_: zeros, pipeline_mode=pl.Buffered(1))


def _rms(x, gain):
    ms = jnp.mean(x * x, axis=-1, keepdims=True)
    return x * lax.rsqrt(ms + EPS) * gain


def _dot(a, b):
    return jnp.dot(a, b, preferred_element_type=F32)


def _dot_nt(a, b):
    return lax.dot_general(a, b, (((1,), (1,)), ((), ())), preferred_element_type=F32)


def _rope(t, c, sa, sb):
    return t * c + pltpu.roll(t, LANES - 32, 1) * sa + pltpu.roll(t, 32, 1) * sb


def _prep_kernel(x_ref, nmix_ref, w1_ref, qn_ref, wuq_ref, kvn_ref, wukv_ref, gq_ref, gk_ref, tab_ref,
                 h_ref, qm_ref, km_ref, vm_ref, qg_ref, kg_ref, vg_ref):
    x = x_ref[0]
    h = _rms(x, nmix_ref[...]).astype(BF16)
    h_ref[0] = h
    z = _dot(h, w1_ref[...])
    c1, sa1, sb1 = tab_ref[0], tab_ref[1], tab_ref[2]
    c2, sa2, sb2 = tab_ref[3], tab_ref[4], tab_ref[5]

    scale_a = (MLA_NOPE + MLA_ROPE) ** -0.5
    cq = _rms(z[:, 0:MLA_Q_RANK], qn_ref[...]).astype(BF16)
    q = _dot(cq, wuq_ref[...])
    for hh in range(MLA_HEADS):
        base = 2 * LANES * hh
        qm_ref[0, hh, :, 0:LANES] = (q[:, base:base + LANES] * scale_a).astype(BF16)
        r = _rope(q[:, base + LANES:base + 2 * LANES], c1, sa1, sb1)
        qm_ref[0, hh, :, LANES:2 * LANES] = (r * scale_a).astype(BF16)

    ckv = _rms(z[:, MLA_Q_RANK:MLA_Q_RANK + MLA_KV_RANK], kvn_ref[...]).astype(BF16)
    kv = _dot(ckv, wukv_ref[...])
    kr = _rope(z[:, 2560:2688], c1, sa1, sb1).astype(BF16)
    for hh in range(MLA_HEADS):
        km_ref[0, hh, :, 0:LANES] = kv[:, LANES * hh:LANES * (hh + 1)].astype(BF16)
        km_ref[0, hh, :, LANES:2 * LANES] = kr
        vb = MLA_HEADS * MLA_NOPE + MLA_V * hh
        vm_ref[0, hh] = kv[:, vb:vb + MLA_V].astype(BF16)

    scale_b = GQA_HEAD_DIM ** -0.5
    for hh in range(GQA_HEADS):
        t = _rms(z[:, 1024 + LANES * hh:1024 + LANES * (hh + 1)], gq_ref[...])
        qg_ref[0, hh] = (_rope(t, c2, sa2, sb2) * scale_b).astype(BF16)
    for hh in range(GQA_KV_HEADS):
        t = _rms(z[:, 2048 + LANES * hh:2048 + LANES * (hh + 1)], gk_ref[...])
        kg_ref[0, hh] = _rope(t, c2, sa2, sb2).astype(BF16)
        vg_ref[0, hh] = z[:, 2304 + LANES * hh:2304 + LANES * (hh + 1)].astype(BF16)


def _prep(x, nmix, w1, qn, wuq, kvn, wukv, gq, gk, tabs, tm):
    B, L, D = x.shape
    grid = (B, L // tm)
    tok = lambda b, i: (b, i, 0)
    head = lambda b, i: (b, 0, i, 0)
    out_shape = (
        jax.ShapeDtypeStruct((B, L, D), BF16),
        jax.ShapeDtypeStruct((B, MLA_HEADS, L, 2 * LANES), BF16),
        jax.ShapeDtypeStruct((B, MLA_HEADS, L, 2 * LANES), BF16),
        jax.ShapeDtypeStruct((B, MLA_HEADS, L, MLA_V), BF16),
        jax.ShapeDtypeStruct((B, GQA_HEADS, L, GQA_HEAD_DIM), BF16),
        jax.ShapeDtypeStruct((B, GQA_KV_HEADS, L, GQA_HEAD_DIM), BF16),
        jax.ShapeDtypeStruct((B, GQA_KV_HEADS, L, GQA_HEAD_DIM), BF16),
    )
    out_specs = (
        pl.BlockSpec((1, tm, D), tok),
        pl.BlockSpec((1, MLA_HEADS, tm, 2 * LANES), head),
        pl.BlockSpec((1, MLA_HEADS, tm, 2 * LANES), head),
        pl.BlockSpec((1, MLA_HEADS, tm, MLA_V), head),
        pl.BlockSpec((1, GQA_HEADS, tm, GQA_HEAD_DIM), head),
        pl.BlockSpec((1, GQA_KV_HEADS, tm, GQA_HEAD_DIM), head),
        pl.BlockSpec((1, GQA_KV_HEADS, tm, GQA_HEAD_DIM), head),
    )
    in_specs = [
        pl.BlockSpec((1, tm, D), tok),
        _resident(nmix.shape), _resident(w1.shape), _resident(qn.shape), _resident(wuq.shape),
        _resident(kvn.shape), _resident(wukv.shape), _resident(gq.shape), _resident(gk.shape),
        pl.BlockSpec((6, tm, LANES), lambda b, i: (0, i, 0)),
    ]
    return pl.pallas_call(
        _prep_kernel, grid=grid, in_specs=in_specs, out_specs=out_specs, out_shape=out_shape,
        compiler_params=_params(("parallel", "parallel")), name="prep",
    )(x, nmix, w1, qn, wuq, kvn, wukv, gq, gk, tabs)


def _gate_kernel(h_ref, w_ref, b_ref, g_ref, *, chunk):
    h = h_ref[...]
    for c in range(w_ref.shape[1] // chunk):
        sl = slice(c * chunk, (c + 1) * chunk)
        pre = _dot(h, w_ref[:, sl]) + b_ref[:, sl]
        g_ref[:, sl] = (1.0 / (1.0 + jnp.exp(-pre))).astype(BF16)


def _gate(h2d, wg, bg, tm):
    T, D = h2d.shape
    N = wg.shape[1]
    return pl.pallas_call(
        functools.partial(_gate_kernel, chunk=1024),
        grid=(T // tm,),
        in_specs=[pl.BlockSpec((tm, D), lambda i: (i, 0)), _resident(wg.shape), _resident(bg.shape)],
        out_specs=pl.BlockSpec((tm, N), lambda i: (i, 0)),
        out_shape=jax.ShapeDtypeStruct((T, N), BF16),
        compiler_params=_params(("parallel",)), name="gate",
    )(h2d, wg, bg)


def _memkv_kernel(mem_ref, nmem_ref, w_ref, o_ref):
    mn = _rms(mem_ref[0], nmem_ref[...]).astype(BF16)
    o_ref[0] = _dot(mn, w_ref[...]).astype(BF16)


def _memkv(mem, nmem, wxkv):
    B, M, D = mem.shape
    N = wxkv.shape[1]
    return pl.pallas_call(
        _memkv_kernel, grid=(B,),
        in_specs=[pl.BlockSpec((1, M, D), lambda b: (b, 0, 0)), _resident(nmem.shape), _resident(wxkv.shape)],
        out_specs=pl.BlockSpec((1, M, N), lambda b: (b, 0, 0)),
        out_shape=jax.ShapeDtypeStruct((B, M, N), BF16),
        compiler_params=_params(("parallel",)), name="memkv",
    )(mem, nmem, wxkv)


def _attn_kernel(q_ref, k_ref, v_ref, o_ref, *, tk):
    _, nh, tq, dk = q_ref.shape
    S, dv = v_ref.shape[2], v_ref.shape[3]
    rows = nh * tq
    q = q_ref[0].reshape(rows, dk)

    def body(c, carry):
        m, l, acc = carry
        start = pl.multiple_of(c * tk, tk)
        ks = k_ref[0, 0, pl.ds(start, tk), :]
        vs = v_ref[0, 0, pl.ds(start, tk), :]
        s = _dot_nt(q, ks)
        m_new = jnp.maximum(m, jnp.max(s, axis=-1, keepdims=True))
        alpha = jnp.exp(m - m_new)
        p = jnp.exp(s - m_new)
        l = alpha * l + jnp.sum(p, axis=-1, keepdims=True)
        acc = alpha * acc + _dot(p.astype(BF16), vs)
        return m_new, l, acc

    init = (jnp.full((rows, 1), NEG_BIG, F32), jnp.zeros((rows, 1), F32), jnp.zeros((rows, dv), F32))
    _, l, acc = lax.fori_loop(0, S // tk, body, init)
    o = acc / l
    for hh in range(nh):
        o_ref[0, :, hh * dv:(hh + 1) * dv] = o[hh * tq:(hh + 1) * tq].astype(BF16)


def _attention(q, k, v, tq, tk):
    B, H, L, dk = q.shape
    G, S, dv = k.shape[1], k.shape[2], v.shape[3]
    nh = H // G
    return pl.pallas_call(
        functools.partial(_attn_kernel, tk=tk),
        grid=(B, G, L // tq),
        in_specs=[
            pl.BlockSpec((1, nh, tq, dk), lambda b, g, i: (b, g, i, 0)),
            pl.BlockSpec((1, 1, S, dk), lambda b, g, i: (b, g, 0, 0)),
            pl.BlockSpec((1, 1, S, dv), lambda b, g, i: (b, g, 0, 0)),
        ],
        out_specs=pl.BlockSpec((1, tq, nh * dv), lambda b, g, i: (b, i, g)),
        out_shape=jax.ShapeDtypeStruct((B, L, H * dv), BF16),
        compiler_params=_params(("parallel", "parallel", "arbitrary")), name="attn",
    )(q, k, v)


def _mix_kernel(x_ref, g_ref, ya_ref, yb_ref, woa_ref, wob_ref, wout_ref, ncross_ref, wxq_ref, kvx_ref,
                wxo_ref, nffn_ref, x2_ref, hf_ref):
    D = x_ref.shape[2]
    a = _dot(ya_ref[0], woa_ref[...])
    b = _dot(yb_ref[0], wob_ref[...])
    m = g_ref[:, 0:D].astype(F32) * a + g_ref[:, D:2 * D].astype(F32) * b
    x1 = x_ref[0] + _dot(m.astype(BF16), wout_ref[...])

    hc = _rms(x1, ncross_ref[...]).astype(BF16)
    q = _dot(hc, wxq_ref[...]) * (X_HEAD_DIM ** -0.5)
    kw = X_HEADS * X_HEAD_DIM
    outs = []
    for hh in range(X_HEADS):
        sl = slice(X_HEAD_DIM * hh, X_HEAD_DIM * (hh + 1))
        s = _dot_nt(q[:, sl].astype(BF16), kvx_ref[0, :, sl])
        p = jnp.exp(s - jnp.max(s, axis=-1, keepdims=True))
        l = jnp.sum(p, axis=-1, keepdims=True)
        o = _dot(p.astype(BF16), kvx_ref[0, :, kw + X_HEAD_DIM * hh:kw + X_HEAD_DIM * (hh + 1)]) / l
        outs.append(o.astype(BF16))
    x2 = x1 + _dot(jnp.concatenate(outs, axis=-1), wxo_ref[...])
    x2_ref[0] = x2
    hf_ref[0] = _rms(x2, nffn_ref[...]).astype(BF16)


def _mix(x, g2d, ya, yb, woa, wob, wout, ncross, wxq, kvx, wxo, nffn, tm):
    B, L, D = x.shape
    nt = L // tm
    tok = lambda b, i: (b, i, 0)
    return pl.pallas_call(
        _mix_kernel, grid=(B, nt),
        in_specs=[
            pl.BlockSpec((1, tm, D), tok),
            pl.BlockSpec((tm, 2 * D), lambda b, i: (b * nt + i, 0)),
            pl.BlockSpec((1, tm, ya.shape[2]), tok),
            pl.BlockSpec((1, tm, yb.shape[2]), tok),
            _resident(woa.shape), _resident(wob.shape), _resident(wout.shape), _resident(ncross.shape),
            _resident(wxq.shape),
            pl.BlockSpec((1,) + kvx.shape[1:], lambda b, i: (b, 0, 0)),
            _resident(wxo.shape), _resident(nffn.shape),
        ],
        out_specs=(pl.BlockSpec((1, tm, D), tok), pl.BlockSpec((1, tm, D), tok)),
        out_shape=(jax.ShapeDtypeStruct((B, L, D), F32), jax.ShapeDtypeStruct((B, L, D), BF16)),
        compiler_params=_params(("parallel", "parallel")), name="mix",
    )(x, g2d, ya, yb, woa, wob, wout, ncross, wxq, kvx, wxo, nffn)


def _ffn_kernel(hm_ref, hp_ref, hn_ref, x2_ref, wa_ref, wb_ref, cwa_ref, cwb_ref, wd_ref, nfin_ref,
                o_ref, hbuf, acc, *, tm):
    i, f = pl.program_id(1), pl.program_id(2)
    halo = BF16_ROWS

    @pl.when(f == 0)
    def _():
        prev, nxt = hp_ref[0], hn_ref[0]
        hbuf[0:halo] = jnp.where(i == 0, jnp.zeros_like(prev), prev)
        hbuf[halo:halo + tm] = hm_ref[0]
        hbuf[halo + tm:] = jnp.where(i == pl.num_programs(1) - 1, jnp.zeros_like(nxt), nxt)
        acc[...] = jnp.zeros_like(acc)

    hb = hbuf[...]
    rows = tm + 2 * halo

    def conv(u, cw):
        um = pltpu.roll(u, 1, 0)[halo:halo + tm]
        up = pltpu.roll(u, rows - 1, 0)[halo:halo + tm]
        return um * cw[0:1] + u[halo:halo + tm] * cw[1:2] + up * cw[2:3] + cw[3:4]

    a = conv(_dot(hb, wa_ref[...]), cwa_ref[...])
    b = conv(_dot(hb, wb_ref[...]), cwb_ref[...])
    act = a * (1.0 / (1.0 + jnp.exp(-a))) * b
    acc[...] += _dot(act.astype(BF16), wd_ref[...])

    @pl.when(f == pl.num_programs(2) - 1)
    def _():
        o_ref[0] = _rms(x2_ref[0] + acc[...], nfin_ref[...])


def _ffn(hf, x2, wup, cw, wdown, nfin, tm, tf):
    B, L, D = hf.shape
    dff = wdown.shape[0]
    nf = dff // tf
    hb = tm // BF16_ROWS
    last = L // BF16_ROWS - 1
    tok = lambda b, i, f: (b, i, 0)
    return pl.pallas_call(
        functools.partial(_ffn_kernel, tm=tm),
        grid=(B, L // tm, nf),
        in_specs=[
            pl.BlockSpec((1, tm, D), tok),
            pl.BlockSpec((1, BF16_ROWS, D), lambda b, i, f: (b, jnp.maximum(i * hb - 1, 0), 0)),
            pl.BlockSpec((1, BF16_ROWS, D), lambda b, i, f: (b, jnp.minimum((i + 1) * hb, last), 0)),
            pl.BlockSpec((1, tm, D), tok),
            pl.BlockSpec((D, tf), lambda b, i, f: (0, f)),
            pl.BlockSpec((D, tf), lambda b, i, f: (0, f + nf)),
            pl.BlockSpec((8, tf), lambda b, i, f: (0, f)),
            pl.BlockSpec((8, tf), lambda b, i, f: (0, f + nf)),
            pl.BlockSpec((tf, D), lambda b, i, f: (f, 0)),
            pl.BlockSpec(nfin.shape, lambda b, i, f: (0, 0)),
        ],
        out_specs=pl.BlockSpec((1, tm, D), tok),
        out_shape=jax.ShapeDtypeStruct((B, L, D), F32),
        scratch_shapes=[pltpu.VMEM((tm + 2 * BF16_ROWS, D), BF16), pltpu.VMEM((tm, D), F32)],
        compiler_params=_params(("parallel", "parallel", "arbitrary")), name="ffn",
    )(hf, hf, hf, x2, wup, wup, cw, cw, wdown, nfin)


def _rope_tables(L):
    half = MLA_ROPE // 2
    inv = jnp.power(ROPE_THETA, -2.0 * jnp.arange(half, dtype=F32) / (2 * half))
    t = jnp.arange(L, dtype=jnp.int32)

    def cs(pos):
        ang = pos.astype(F32)[:, None] * inv[None, :]
        return jnp.cos(ang), jnp.sin(ang)

    z = jnp.zeros((L, half), F32)
    ct, st = cs(t)
    cr, sr = cs(t // GRID_W)
    cc, sc = cs(t % GRID_W)
    cat = lambda *p: jnp.concatenate(p, axis=-1)
    return jnp.stack([
        cat(ct, ct, z, z), cat(-st, z, z, z), cat(z, st, z, z),
        cat(cr, cr, cc, cc), cat(-sr, z, -sc, z), cat(z, sr, z, sc),
    ])


def kernel(x, mem, norm_mix, w_in, mla_q_norm, w_uq, mla_kv_norm, w_ukv, gqa_q_norm, gqa_k_norm, w_o_mla,
           w_o_gqa, w_gate, b_gate, w_out, norm_cross, norm_mem, w_xq, w_xkv, w_xo, norm_ffn, w_up, conv_w,
           conv_b, w_down, norm_final):
    B, L, D = x.shape
    depth = w_in.shape[0]
    assert depth == 1, "the final norm is fused into the (single) layer's FFN kernel"
    tabs = _rope_tables(L)
    row = lambda v: v.reshape(1, -1).astype(F32)
    tm_prep = min(256, L)
    tm_gate = min(256, L)
    tm_mix = min(256, L)
    tm_ffn = min(512, L)
    tq_a = min(512, L)
    tq_b = min(128, L)
    tk = min(512, L)

    for l in range(depth):
        wi = w_in[l]
        o_cq, o_ckv = 0, MLA_Q_RANK
        o_kr = o_ckv + MLA_KV_RANK
        o_qb = o_kr + MLA_ROPE
        o_kb = o_qb + GQA_HEADS * GQA_HEAD_DIM
        o_vb = o_kb + GQA_KV_HEADS * GQA_HEAD_DIM
        w1 = jnp.concatenate([
            wi[:, o_cq:o_kr], wi[:, o_qb:], wi[:, o_kr:o_qb], jnp.zeros((D, LANES - MLA_ROPE), wi.dtype),
        ], axis=1).astype(BF16)
        assert o_vb + GQA_KV_HEADS * GQA_HEAD_DIM == wi.shape[1]
        wuq = jnp.pad(w_uq[l].reshape(MLA_Q_RANK, MLA_HEADS, MLA_NOPE + MLA_ROPE),
                      ((0, 0), (0, 0), (0, 2 * LANES - MLA_NOPE - MLA_ROPE))
                      ).reshape(MLA_Q_RANK, MLA_HEADS * 2 * LANES).astype(BF16)
        wkv = w_ukv[l].reshape(MLA_KV_RANK, MLA_HEADS, MLA_NOPE + MLA_V)
        wukv = jnp.concatenate([wkv[:, :, :MLA_NOPE].reshape(MLA_KV_RANK, -1),
                                wkv[:, :, MLA_NOPE:].reshape(MLA_KV_RANK, -1)], axis=1).astype(BF16)
        wx = w_xkv[l].reshape(D, X_HEADS, 2 * X_HEAD_DIM)
        wxkv = jnp.concatenate([wx[:, :, :X_HEAD_DIM].reshape(D, -1),
                                wx[:, :, X_HEAD_DIM:].reshape(D, -1)], axis=1).astype(BF16)
        cw = jnp.concatenate([conv_w[l], conv_b[l][None, :],
                              jnp.zeros((8 - CONV_W - 1, conv_w.shape[2]), F32)], axis=0)

        h, qm, km, vm, qg, kg, vg = _prep(
            x, row(norm_mix[l]), w1, row(mla_q_norm[l]), wuq, row(mla_kv_norm[l]), wukv,
            row(gqa_q_norm[l]), row(gqa_k_norm[l]), tabs, tm_prep)
        g = _gate(h.reshape(B * L, D), w_gate[l].astype(BF16), row(b_gate[l]), tm_gate)
        ya = _attention(qm, km, vm, tq_a, tk)
        yb = _attention(qg, kg, vg, tq_b, tk)
        kvx = _memkv(mem, row(norm_mem[l]), wxkv)
        x2, hf = _mix(x, g, ya, yb, w_o_mla[l].astype(BF16), w_o_gqa[l].astype(BF16), w_out[l].astype(BF16),
                      row(norm_cross[l]), w_xq[l].astype(BF16), kvx, w_xo[l].astype(BF16), row(norm_ffn[l]),
                      tm_mix)
        x = _ffn(hf, x2, w_up[l].astype(BF16), cw, w_down[l].astype(BF16), row(norm_final), tm_ffn,
                 min(512, w_down.shape[1]))
    return x
```

```python
import functools

import jax
import jax.numpy as jnp
import numpy as np
from jax import lax
from jax.experimental import pallas as pl
from jax.experimental.pallas import tpu as pltpu

F32 = jnp.float32
BF16 = jnp.bfloat16

EPS = 1e-6
ROPE_THETA = 10000.0
GRID_W = 64
MLA_HEADS = 8
MLA_Q_RANK = 512
MLA_KV_RANK = 512
MLA_NOPE = 128
MLA_ROPE = 64
MLA_V = 128
GQA_HEADS = 8
GQA_KV_HEADS = 2
GQA_HEAD_DIM = 128
X_HEADS = 4
X_HEAD_DIM = 128
CONV_W = 3

LANES = 128
BF16_ROWS = 16
VMEM_LIMIT = 56 * 1024 * 1024
NEG_BIG = -1e30
LOG2E = float(np.log2(np.e))


def _params(semantics):
    return pltpu.CompilerParams(dimension_semantics=semantics, vmem_limit_bytes=VMEM_LIMIT)


def _resident(shape):
    zeros = (0,) * len(shape)
    return pl.BlockSpec(shape, lambda *_: zeros, pipeline_mode=pl.Buffered(1))


def _rms(x, gain):
    ms = jnp.mean(x * x, axis=-1, keepdims=True)
    return x * lax.rsqrt(ms + EPS) * gain


def _dot(a, b):
    return jnp.dot(a, b, preferred_element_type=F32)


def _dot_nt(a, b):
    return lax.dot_general(a, b, (((1,), (1,)), ((), ())), preferred_element_type=F32)


def _rope(t, c, sa, sb):
    return t * c + pltpu.roll(t, LANES - 32, 1) * sa + pltpu.roll(t, 32, 1) * sb


def _prep_kernel(x_ref, nmix_ref, w1_ref, qn_ref, wuq_ref, kvn_ref, wukv_ref, gq_ref, gk_ref, tab_ref,
                 h_ref, qm_ref, km_ref, vm_ref, qg_ref, kg_ref, vg_ref):
    x = x_ref[0]
    h = _rms(x, nmix_ref[...]).astype(BF16)
    h_ref[0] = h
    z = _dot(h, w1_ref[...])
    c1, sa1, sb1 = tab_ref[0], tab_ref[1], tab_ref[2]
    c2, sa2, sb2 = tab_ref[3], tab_ref[4], tab_ref[5]

    scale_a = (MLA_NOPE + MLA_ROPE) ** -0.5 * LOG2E
    cq = _rms(z[:, 0:MLA_Q_RANK], qn_ref[...]).astype(BF16)
    q = _dot(cq, wuq_ref[...])
    for hh in range(MLA_HEADS):
        base = 2 * LANES * hh
        qm_ref[0, hh, :, 0:LANES] = (q[:, base:base + LANES] * scale_a).astype(BF16)
        r = _rope(q[:, base + LANES:base + 2 * LANES], c1, sa1, sb1)
        qm_ref[0, hh, :, LANES:2 * LANES] = (r * scale_a).astype(BF16)

    ckv = _rms(z[:, MLA_Q_RANK:MLA_Q_RANK + MLA_KV_RANK], kvn_ref[...]).astype(BF16)
    kv = _dot(ckv, wukv_ref[...])
    kr = _rope(z[:, 2560:2688], c1, sa1, sb1).astype(BF16)
    for hh in range(MLA_HEADS):
        km_ref[0, hh, :, 0:LANES] = kv[:, LANES * hh:LANES * (hh + 1)].astype(BF16)
        km_ref[0, hh, :, LANES:2 * LANES] = kr
        vb = MLA_HEADS * MLA_NOPE + MLA_V * hh
        vm_ref[0, hh] = kv[:, vb:vb + MLA_V].astype(BF16)

    scale_b = GQA_HEAD_DIM ** -0.5 * LOG2E
    for hh in range(GQA_HEADS):
        t = _rms(z[:, 1024 + LANES * hh:1024 + LANES * (hh + 1)], gq_ref[...])
        qg_ref[0, hh] = (_rope(t, c2, sa2, sb2) * scale_b).astype(BF16)
    for hh in range(GQA_KV_HEADS):
        t = _rms(z[:, 2048 + LANES * hh:2048 + LANES * (hh + 1)], gk_ref[...])
        kg_ref[0, hh] = _rope(t, c2, sa2, sb2).astype(BF16)
        vg_ref[0, hh] = z[:, 2304 + LANES * hh:2304 + LANES * (hh + 1)].astype(BF16)


def _prep(x, nmix, w1, qn, wuq, kvn, wukv, gq, gk, tabs, tm):
    B, L, D = x.shape
    grid = (B, L // tm)
    tok = lambda b, i: (b, i, 0)
    head = lambda b, i: (b, 0, i, 0)
    out_shape = (
        jax.ShapeDtypeStruct((B, L, D), BF16),
        jax.ShapeDtypeStruct((B, MLA_HEADS, L, 2 * LANES), BF16),
        jax.ShapeDtypeStruct((B, MLA_HEADS, L, 2 * LANES), BF16),
        jax.ShapeDtypeStruct((B, MLA_HEADS, L, MLA_V), BF16),
        jax.ShapeDtypeStruct((B, GQA_HEADS, L, GQA_HEAD_DIM), BF16),
        jax.ShapeDtypeStruct((B, GQA_KV_HEADS, L, GQA_HEAD_DIM), BF16),
        jax.ShapeDtypeStruct((B, GQA_KV_HEADS, L, GQA_HEAD_DIM), BF16),
    )
    out_specs = (
        pl.BlockSpec((1, tm, D), tok),
        pl.BlockSpec((1, MLA_HEADS, tm, 2 * LANES), head),
        pl.BlockSpec((1, MLA_HEADS, tm, 2 * LANES), head),
        pl.BlockSpec((1, MLA_HEADS, tm, MLA_V), head),
        pl.BlockSpec((1, GQA_HEADS, tm, GQA_HEAD_DIM), head),
        pl.BlockSpec((1, GQA_KV_HEADS, tm, GQA_HEAD_DIM), head),
        pl.BlockSpec((1, GQA_KV_HEADS, tm, GQA_HEAD_DIM), head),
    )
    in_specs = [
        pl.BlockSpec((1, tm, D), tok),
        _resident(nmix.shape), _resident(w1.shape), _resident(qn.shape), _resident(wuq.shape),
        _resident(kvn.shape), _resident(wukv.shape), _resident(gq.shape), _resident(gk.shape),
        pl.BlockSpec((6, tm, LANES), lambda b, i: (0, i, 0)),
    ]
    return pl.pallas_call(
        _prep_kernel, grid=grid, in_specs=in_specs, out_specs=out_specs, out_shape=out_shape,
        compiler_params=_params(("parallel", "parallel")), name="prep",
    )(x, nmix, w1, qn, wuq, kvn, wukv, gq, gk, tabs)


def _gate_kernel(h_ref, w_ref, b_ref, g_ref, *, chunk):
    h = h_ref[...]
    for c in range(w_ref.shape[1] // chunk):
        sl = slice(c * chunk, (c + 1) * chunk)
        pre = _dot(h, w_ref[:, sl]) + b_ref[:, sl]
        g_ref[:, sl] = (1.0 / (1.0 + jnp.exp(-pre))).astype(BF16)


def _gate(h2d, wg, bg, tm):
    T, D = h2d.shape
    N = wg.shape[1]
    return pl.pallas_call(
        functools.partial(_gate_kernel, chunk=1024),
        grid=(T // tm,),
        in_specs=[pl.BlockSpec((tm, D), lambda i: (i, 0)), _resident(wg.shape), _resident(bg.shape)],
        out_specs=pl.BlockSpec((tm, N), lambda i: (i, 0)),
        out_shape=jax.ShapeDtypeStruct((T, N), BF16),
        compiler_params=_params(("parallel",)), name="gate",
    )(h2d, wg, bg)


def _memkv_kernel(mem_ref, nmem_ref, w_ref, o_ref):
    mn = _rms(mem_ref[0], nmem_ref[...]).astype(BF16)
    o_ref[0] = _dot(mn, w_ref[...]).astype(BF16)


def _memkv(mem, nmem, wxkv):
    B, M, D = mem.shape
    N = wxkv.shape[1]
    return pl.pallas_call(
        _memkv_kernel, grid=(B,),
        in_specs=[pl.BlockSpec((1, M, D), lambda b: (b, 0, 0)), _resident(nmem.shape), _resident(wxkv.shape)],
        out_specs=pl.BlockSpec((1, M, N), lambda b: (b, 0, 0)),
        out_shape=jax.ShapeDtypeStruct((B, M, N), BF16),
        compiler_params=_params(("parallel",)), name="memkv",
    )(mem, nmem, wxkv)


def _attn_kernel(q_ref, k_ref, v_ref, o_ref, s_a, s_b, m_s, l_s, acc_s, *, tk):
    _, nh, tq, dk = q_ref.shape
    S, dv = v_ref.shape[2], v_ref.shape[3]
    rows = nh * tq
    nchunks = S // tk
    ncol = tk // LANES
    q = q_ref[0].reshape(rows, dk)
    assert dv == LANES, "alpha (lane-replicated) is applied to the accumulator without a broadcast"

    s_bufs = (s_a, s_b)

    def chunk(c):
        return pl.ds(c * tk, tk)

    def scores(c):
        return _dot_nt(q, k_ref[0, 0, chunk(c), :])

    def step(c):
        s_cur, s_nxt = s_bufs[c % 2], s_bufs[(c + 1) % 2]
        if c + 1 < nchunks:
            s_nxt[...] = scores(c + 1)
        cols = [s_cur[:, j * LANES:(j + 1) * LANES] for j in range(ncol)]
        mc = functools.reduce(jnp.maximum, cols)
        m_prev = m_s[...]
        m_new = jnp.maximum(m_prev, jnp.max(mc, axis=-1, keepdims=True))
        alpha = jnp.exp2(m_prev - m_new)
        ps = [jnp.exp2(cj - m_new) for cj in cols]
        l_s[...] = alpha * l_s[...] + functools.reduce(jnp.add, ps)
        p = jnp.concatenate([pj.astype(BF16) for pj in ps], axis=-1)
        acc_s[...] = alpha * acc_s[...] + _dot(p, v_ref[0, 0, chunk(c), :])
        m_s[...] = m_new

    m_s[...] = jnp.full(m_s.shape, NEG_BIG, F32)
    l_s[...] = jnp.zeros(l_s.shape, F32)
    acc_s[...] = jnp.zeros(acc_s.shape, F32)
    s_bufs[0][...] = scores(0)
    for c in range(nchunks):
        step(c)
    o = acc_s[...] / jnp.sum(l_s[...], axis=-1, keepdims=True)
    for hh in range(nh):
        o_ref[0, :, hh * dv:(hh + 1) * dv] = o[hh * tq:(hh + 1) * tq].astype(BF16)


def _attention(q, k, v, tq, tk):
    B, H, L, dk = q.shape
    G, S, dv = k.shape[1], k.shape[2], v.shape[3]
    nh = H // G
    return pl.pallas_call(
        functools.partial(_attn_kernel, tk=tk),
        grid=(B, G, L // tq),
        in_specs=[
            pl.BlockSpec((1, nh, tq, dk), lambda b, g, i: (b, g, i, 0)),
            pl.BlockSpec((1, 1, S, dk), lambda b, g, i: (b, g, 0, 0)),
            pl.BlockSpec((1, 1, S, dv), lambda b, g, i: (b, g, 0, 0)),
        ],
        out_specs=pl.BlockSpec((1, tq, nh * dv), lambda b, g, i: (b, i, g)),
        out_shape=jax.ShapeDtypeStruct((B, L, H * dv), BF16),
        scratch_shapes=[
            pltpu.VMEM((nh * tq, tk), F32),
            pltpu.VMEM((nh * tq, tk), F32),
            pltpu.VMEM((nh * tq, LANES), F32),
            pltpu.VMEM((nh * tq, LANES), F32),
            pltpu.VMEM((nh * tq, dv), F32),
        ],
        compiler_params=_params(("parallel", "parallel", "arbitrary")), name="attn",
    )(q, k, v)


def _mix_kernel(x_ref, g_ref, ya_ref, yb_ref, woa_ref, wob_ref, wout_ref, ncross_ref, wxq_ref, kvx_ref,
                wxo_ref, nffn_ref, x2_ref, hf_ref):
    D = x_ref.shape[2]
    a = _dot(ya_ref[0], woa_ref[...])
    b = _dot(yb_ref[0], wob_ref[...])
    m = g_ref[:, 0:D].astype(F32) * a + g_ref[:, D:2 * D].astype(F32) * b
    x1 = x_ref[0] + _dot(m.astype(BF16), wout_ref[...])

    hc = _rms(x1, ncross_ref[...]).astype(BF16)
    q = _dot(hc, wxq_ref[...]) * (X_HEAD_DIM ** -0.5)
    kw = X_HEADS * X_HEAD_DIM
    outs = []
    for hh in range(X_HEADS):
        sl = slice(X_HEAD_DIM * hh, X_HEAD_DIM * (hh + 1))
        s = _dot_nt(q[:, sl].astype(BF16), kvx_ref[0, :, sl])
        p = jnp.exp(s - jnp.max(s, axis=-1, keepdims=True))
        l = jnp.sum(p, axis=-1, keepdims=True)
        o = _dot(p.astype(BF16), kvx_ref[0, :, kw + X_HEAD_DIM * hh:kw + X_HEAD_DIM * (hh + 1)]) / l
        outs.append(o.astype(BF16))
    x2 = x1 + _dot(jnp.concatenate(outs, axis=-1), wxo_ref[...])
    x2_ref[0] = x2
    hf_ref[0] = _rms(x2, nffn_ref[...]).astype(BF16)


def _mix(x, g2d, ya, yb, woa, wob, wout, ncross, wxq, kvx, wxo, nffn, tm):
    B, L, D = x.shape
    nt = L // tm
    tok = lambda b, i: (b, i, 0)
    return pl.pallas_call(
        _mix_kernel, grid=(B, nt),
        in_specs=[
            pl.BlockSpec((1, tm, D), tok),
            pl.BlockSpec((tm, 2 * D), lambda b, i: (b * nt + i, 0)),
            pl.BlockSpec((1, tm, ya.shape[2]), tok),
            pl.BlockSpec((1, tm, yb.shape[2]), tok),
            _resident(woa.shape), _resident(wob.shape), _resident(wout.shape), _resident(ncross.shape),
            _resident(wxq.shape),
            pl.BlockSpec((1,) + kvx.shape[1:], lambda b, i: (b, 0, 0)),
            _resident(wxo.shape), _resident(nffn.shape),
        ],
        out_specs=(pl.BlockSpec((1, tm, D), tok), pl.BlockSpec((1, tm, D), tok)),
        out_shape=(jax.ShapeDtypeStruct((B, L, D), F32), jax.ShapeDtypeStruct((B, L, D), BF16)),
        compiler_params=_params(("parallel", "parallel")), name="mix",
    )(x, g2d, ya, yb, woa, wob, wout, ncross, wxq, kvx, wxo, nffn)


def _ffn_kernel(hm_ref, hp_ref, hn_ref, x2_ref, wa_ref, wb_ref, cwa_ref, cwb_ref, wd_ref, nfin_ref,
                o_ref, hbuf, acc, *, tm):
    i, f = pl.program_id(1), pl.program_id(2)
    halo = BF16_ROWS

    @pl.when(f == 0)
    def _():
        prev, nxt = hp_ref[0], hn_ref[0]
        hbuf[0:halo] = jnp.where(i == 0, jnp.zeros_like(prev), prev)
        hbuf[halo:halo + tm] = hm_ref[0]
        hbuf[halo + tm:] = jnp.where(i == pl.num_programs(1) - 1, jnp.zeros_like(nxt), nxt)
        acc[...] = jnp.zeros_like(acc)

    hb = hbuf[...]
    rows = tm + 2 * halo

    def conv(u, cw):
        um = pltpu.roll(u, 1, 0)[halo:halo + tm]
        up = pltpu.roll(u, rows - 1, 0)[halo:halo + tm]
        return um * cw[0:1] + u[halo:halo + tm] * cw[1:2] + up * cw[2:3] + cw[3:4]

    a = conv(_dot(hb, wa_ref[...]), cwa_ref[...])
    b = conv(_dot(hb, wb_ref[...]), cwb_ref[...])
    act = a * (1.0 / (1.0 + jnp.exp(-a))) * b
    acc[...] += _dot(act.astype(BF16), wd_ref[...])

    @pl.when(f == pl.num_programs(2) - 1)
    def _():
        o_ref[0] = _rms(x2_ref[0] + acc[...], nfin_ref[...])


def _ffn(hf, x2, wup, cw, wdown, nfin, tm, tf):
    B, L, D = hf.shape
    dff = wdown.shape[0]
    nf = dff // tf
    hb = tm // BF16_ROWS
    last = L // BF16_ROWS - 1
    tok = lambda b, i, f: (b, i, 0)
    return pl.pallas_call(
        functools.partial(_ffn_kernel, tm=tm),
        grid=(B, L // tm, nf),
        in_specs=[
            pl.BlockSpec((1, tm, D), tok),
            pl.BlockSpec((1, BF16_ROWS, D), lambda b, i, f: (b, jnp.maximum(i * hb - 1, 0), 0)),
            pl.BlockSpec((1, BF16_ROWS, D), lambda b, i, f: (b, jnp.minimum((i + 1) * hb, last), 0)),
            pl.BlockSpec((1, tm, D), tok),
            pl.BlockSpec((D, tf), lambda b, i, f: (0, f)),
            pl.BlockSpec((D, tf), lambda b, i, f: (0, f + nf)),
            pl.BlockSpec((8, tf), lambda b, i, f: (0, f)),
            pl.BlockSpec((8, tf), lambda b, i, f: (0, f + nf)),
            pl.BlockSpec((tf, D), lambda b, i, f: (f, 0)),
            pl.BlockSpec(nfin.shape, lambda b, i, f: (0, 0)),
        ],
        out_specs=pl.BlockSpec((1, tm, D), tok),
        out_shape=jax.ShapeDtypeStruct((B, L, D), F32),
        scratch_shapes=[pltpu.VMEM((tm + 2 * BF16_ROWS, D), BF16), pltpu.VMEM((tm, D), F32)],
        compiler_params=_params(("parallel", "parallel", "arbitrary")), name="ffn",
    )(hf, hf, hf, x2, wup, wup, cw, cw, wdown, nfin)


def _rope_tables(L):
    half = MLA_ROPE // 2
    inv = jnp.power(ROPE_THETA, -2.0 * jnp.arange(half, dtype=F32) / (2 * half))
    t = jnp.arange(L, dtype=jnp.int32)

    def cs(pos):
        ang = pos.astype(F32)[:, None] * inv[None, :]
        return jnp.cos(ang), jnp.sin(ang)

    z = jnp.zeros((L, half), F32)
    ct, st = cs(t)
    cr, sr = cs(t // GRID_W)
    cc, sc = cs(t % GRID_W)
    cat = lambda *p: jnp.concatenate(p, axis=-1)
    return jnp.stack([
        cat(ct, ct, z, z), cat(-st, z, z, z), cat(z, st, z, z),
        cat(cr, cr, cc, cc), cat(-sr, z, -sc, z), cat(z, sr, z, sc),
    ])


def kernel(x, mem, norm_mix, w_in, mla_q_norm, w_uq, mla_kv_norm, w_ukv, gqa_q_norm, gqa_k_norm, w_o_mla,
           w_o_gqa, w_gate, b_gate, w_out, norm_cross, norm_mem, w_xq, w_xkv, w_xo, norm_ffn, w_up, conv_w,
           conv_b, w_down, norm_final):
    B, L, D = x.shape
    depth = w_in.shape[0]
    assert depth == 1, "the final norm is fused into the (single) layer's FFN kernel"
    tabs = _rope_tables(L)
    row = lambda v: v.reshape(1, -1).astype(F32)
    tm_prep = min(256, L)
    tm_gate = min(256, L)
    tm_mix = min(256, L)
    tm_ffn = min(512, L)
    tq_a = min(512, L)
    tq_b = min(128, L)
    tk = min(512, L)

    for l in range(depth):
        wi = w_in[l]
        o_cq, o_ckv = 0, MLA_Q_RANK
        o_kr = o_ckv + MLA_KV_RANK
        o_qb = o_kr + MLA_ROPE
        o_kb = o_qb + GQA_HEADS * GQA_HEAD_DIM
        o_vb = o_kb + GQA_KV_HEADS * GQA_HEAD_DIM
        w1 = jnp.concatenate([
            wi[:, o_cq:o_kr], wi[:, o_qb:], wi[:, o_kr:o_qb], jnp.zeros((D, LANES - MLA_ROPE), wi.dtype),
        ], axis=1).astype(BF16)
        assert o_vb + GQA_KV_HEADS * GQA_HEAD_DIM == wi.shape[1]
        wuq = jnp.pad(w_uq[l].reshape(MLA_Q_RANK, MLA_HEADS, MLA_NOPE + MLA_ROPE),
                      ((0, 0), (0, 0), (0, 2 * LANES - MLA_NOPE - MLA_ROPE))
                      ).reshape(MLA_Q_RANK, MLA_HEADS * 2 * LANES).astype(BF16)
        wkv = w_ukv[l].reshape(MLA_KV_RANK, MLA_HEADS, MLA_NOPE + MLA_V)
        wukv = jnp.concatenate([wkv[:, :, :MLA_NOPE].reshape(MLA_KV_RANK, -1),
                                wkv[:, :, MLA_NOPE:].reshape(MLA_KV_RANK, -1)], axis=1).astype(BF16)
        wx = w_xkv[l].reshape(D, X_HEADS, 2 * X_HEAD_DIM)
        wxkv = jnp.concatenate([wx[:, :, :X_HEAD_DIM].reshape(D, -1),
                                wx[:, :, X_HEAD_DIM:].reshape(D, -1)], axis=1).astype(BF16)
        cw = jnp.concatenate([conv_w[l], conv_b[l][None, :],
                              jnp.zeros((8 - CONV_W - 1, conv_w.shape[2]), F32)], axis=0)

        h, qm, km, vm, qg, kg, vg = _prep(
            x, row(norm_mix[l]), w1, row(mla_q_norm[l]), wuq, row(mla_kv_norm[l]), wukv,
            row(gqa_q_norm[l]), row(gqa_k_norm[l]), tabs, tm_prep)
        g = _gate(h.reshape(B * L, D), w_gate[l].astype(BF16), row(b_gate[l]), tm_gate)
        ya = _attention(qm, km, vm, tq_a, tk)
        yb = _attention(qg, kg, vg, tq_b, tk)
        kvx = _memkv(mem, row(norm_mem[l]), wxkv)
        x2, hf = _mix(x, g, ya, yb, w_o_mla[l].astype(BF16), w_o_gqa[l].astype(BF16), w_out[l].astype(BF16),
                      row(norm_cross[l]), w_xq[l].astype(BF16), kvx, w_xo[l].astype(BF16), row(norm_ffn[l]),
                      tm_mix)
        x = _ffn(hf, x2, w_up[l].astype(BF16), cw, w_down[l].astype(BF16), row(norm_final), tm_ffn,
                 min(512, w_down.shape[1]))
    return x
```

```python
import functools

import jax
import jax.numpy as jnp
import numpy as np
from jax import lax
from jax.experimental import pallas as pl
from jax.experimental.pallas import tpu as pltpu

F32 = jnp.float32
BF16 = jnp.bfloat16

EPS = 1e-6
ROPE_THETA = 10000.0
GRID_W = 64
MLA_HEADS = 8
MLA_Q_RANK = 512
MLA_KV_RANK = 512
MLA_NOPE = 128
MLA_ROPE = 64
MLA_V = 128
GQA_HEADS = 8
GQA_KV_HEADS = 2
GQA_HEAD_DIM = 128
X_HEADS = 4
X_HEAD_DIM = 128
CONV_W = 3

LANES = 128
BF16_ROWS = 16
VMEM_LIMIT = 56 * 1024 * 1024
NEG_BIG = -1e30
LOG2E = float(np.log2(np.e))


def _params(semantics):
    return pltpu.CompilerParams(dimension_semantics=semantics, vmem_limit_bytes=VMEM_LIMIT)


def _resident(shape):
    zeros = (0,) * len(shape)
    return pl.BlockSpec(shape, lambda *_: zeros, pipeline_mode=pl.Buffered(1))


def _rms(x, gain):
    ms = jnp.mean(x * x, axis=-1, keepdims=True)
    return x * lax.rsqrt(ms + EPS) * gain


def _dot(a, b):
    return jnp.dot(a, b, preferred_element_type=F32)


def _dot_nt(a, b):
    return lax.dot_general(a, b, (((1,), (1,)), ((), ())), preferred_element_type=F32)


def _rope(t, c, sa, sb):
    return t * c + pltpu.roll(t, LANES - 32, 1) * sa + pltpu.roll(t, 32, 1) * sb


def _prep_kernel(x_ref, nmix_ref, wlat_ref, wgqa_ref, wkr_ref, qn_ref, wuq_ref, kvn_ref, wukv_ref, gq_ref,
                 gk_ref, tab_ref, h_ref, qm_ref, km_ref, vm_ref, qg_ref, kg_ref, vg_ref):
    x = x_ref[0]
    h = _rms(x, nmix_ref[...]).astype(BF16)
    h_ref[0] = h
    zl = _dot(h, wlat_ref[...])
    zg = _dot(h, wgqa_ref[...])
    zr = _dot(h, wkr_ref[...])
    c1, sa1, sb1 = tab_ref[0], tab_ref[1], tab_ref[2]
    c2, sa2, sb2 = tab_ref[3], tab_ref[4], tab_ref[5]

    scale_a = (MLA_NOPE + MLA_ROPE) ** -0.5 * LOG2E
    cq = _rms(zl[:, 0:MLA_Q_RANK], qn_ref[...]).astype(BF16)
    q = _dot(cq, wuq_ref[...])
    for hh in range(MLA_HEADS):
        base = 2 * LANES * hh
        qm_ref[0, hh, :, 0:LANES] = (q[:, base:base + LANES] * scale_a).astype(BF16)
        r = _rope(q[:, base + LANES:base + 2 * LANES], c1, sa1, sb1)
        qm_ref[0, hh, :, LANES:2 * LANES] = (r * scale_a).astype(BF16)

    ckv = _rms(zl[:, MLA_Q_RANK:MLA_Q_RANK + MLA_KV_RANK], kvn_ref[...]).astype(BF16)
    kv = _dot(ckv, wukv_ref[...])
    kr = _rope(zr, c1, sa1, sb1).astype(BF16)
    for hh in range(MLA_HEADS):
        km_ref[0, hh, :, 0:LANES] = kv[:, LANES * hh:LANES * (hh + 1)].astype(BF16)
        km_ref[0, hh, :, LANES:2 * LANES] = kr
        vb = MLA_HEADS * MLA_NOPE + MLA_V * hh
        vm_ref[0, hh] = kv[:, vb:vb + MLA_V].astype(BF16)

    scale_b = GQA_HEAD_DIM ** -0.5 * LOG2E
    dh = GQA_HEAD_DIM
    o_k = GQA_HEADS * dh
    o_v = o_k + GQA_KV_HEADS * dh
    for hh in range(GQA_HEADS):
        t = _rms(zg[:, dh * hh:dh * (hh + 1)], gq_ref[...])
        qg_ref[0, hh] = (_rope(t, c2, sa2, sb2) * scale_b).astype(BF16)
    for hh in range(GQA_KV_HEADS):
        t = _rms(zg[:, o_k + dh * hh:o_k + dh * (hh + 1)], gk_ref[...])
        kg_ref[0, hh] = _rope(t, c2, sa2, sb2).astype(BF16)
        vg_ref[0, hh] = zg[:, o_v + dh * hh:o_v + dh * (hh + 1)].astype(BF16)


def _prep(x, nmix, wlat, wgqa, wkr, qn, wuq, kvn, wukv, gq, gk, tabs, tm):
    B, L, D = x.shape
    grid = (B, L // tm)
    tok = lambda b, i: (b, i, 0)
    head = lambda b, i: (b, 0, i, 0)
    out_shape = (
        jax.ShapeDtypeStruct((B, L, D), BF16),
        jax.ShapeDtypeStruct((B, MLA_HEADS, L, 2 * LANES), BF16),
        jax.ShapeDtypeStruct((B, MLA_HEADS, L, 2 * LANES), BF16),
        jax.ShapeDtypeStruct((B, MLA_HEADS, L, MLA_V), BF16),
        jax.ShapeDtypeStruct((B, GQA_HEADS, L, GQA_HEAD_DIM), BF16),
        jax.ShapeDtypeStruct((B, GQA_KV_HEADS, L, GQA_HEAD_DIM), BF16),
        jax.ShapeDtypeStruct((B, GQA_KV_HEADS, L, GQA_HEAD_DIM), BF16),
    )
    out_specs = (
        pl.BlockSpec((1, tm, D), tok),
        pl.BlockSpec((1, MLA_HEADS, tm, 2 * LANES), head),
        pl.BlockSpec((1, MLA_HEADS, tm, 2 * LANES), head),
        pl.BlockSpec((1, MLA_HEADS, tm, MLA_V), head),
        pl.BlockSpec((1, GQA_HEADS, tm, GQA_HEAD_DIM), head),
        pl.BlockSpec((1, GQA_KV_HEADS, tm, GQA_HEAD_DIM), head),
        pl.BlockSpec((1, GQA_KV_HEADS, tm, GQA_HEAD_DIM), head),
    )
    in_specs = [
        pl.BlockSpec((1, tm, D), tok),
        _resident(nmix.shape), _resident(wlat.shape), _resident(wgqa.shape), _resident(wkr.shape),
        _resident(qn.shape), _resident(wuq.shape),
        _resident(kvn.shape), _resident(wukv.shape), _resident(gq.shape), _resident(gk.shape),
        pl.BlockSpec((6, tm, LANES), lambda b, i: (0, i, 0)),
    ]
    return pl.pallas_call(
        _prep_kernel, grid=grid, in_specs=in_specs, out_specs=out_specs, out_shape=out_shape,
        compiler_params=_params(("parallel", "parallel")), name="prep",
    )(x, nmix, wlat, wgqa, wkr, qn, wuq, kvn, wukv, gq, gk, tabs)


def _gate_kernel(h_ref, w_ref, b_ref, g_ref, *, chunk):
    h = h_ref[...]
    for c in range(w_ref.shape[1] // chunk):
        sl = slice(c * chunk, (c + 1) * chunk)
        pre = _dot(h, w_ref[:, sl]) + b_ref[:, sl]
        g_ref[:, sl] = (1.0 / (1.0 + jnp.exp(-pre))).astype(BF16)


def _gate(h2d, wg, bg, tm):
    T, D = h2d.shape
    N = wg.shape[1]
    return pl.pallas_call(
        functools.partial(_gate_kernel, chunk=1024),
        grid=(T // tm,),
        in_specs=[pl.BlockSpec((tm, D), lambda i: (i, 0)), _resident(wg.shape), _resident(bg.shape)],
        out_specs=pl.BlockSpec((tm, N), lambda i: (i, 0)),
        out_shape=jax.ShapeDtypeStruct((T, N), BF16),
        compiler_params=_params(("parallel",)), name="gate",
    )(h2d, wg, bg)


def _memkv_kernel(mem_ref, nmem_ref, w_ref, o_ref):
    mn = _rms(mem_ref[0], nmem_ref[...]).astype(BF16)
    o_ref[0] = _dot(mn, w_ref[...]).astype(BF16)


def _memkv(mem, nmem, wxkv):
    B, M, D = mem.shape
    N = wxkv.shape[1]
    return pl.pallas_call(
        _memkv_kernel, grid=(B,),
        in_specs=[pl.BlockSpec((1, M, D), lambda b: (b, 0, 0)), _resident(nmem.shape), _resident(wxkv.shape)],
        out_specs=pl.BlockSpec((1, M, N), lambda b: (b, 0, 0)),
        out_shape=jax.ShapeDtypeStruct((B, M, N), BF16),
        compiler_params=_params(("parallel",)), name="memkv",
    )(mem, nmem, wxkv)


def _attn_kernel(q_ref, k_ref, v_ref, o_ref, s_a, s_b, m_s, l_s, acc_s, *, tk):
    _, nh, tq, dk = q_ref.shape
    S, dv = v_ref.shape[2], v_ref.shape[3]
    rows = nh * tq
    nchunks = S // tk
    ncol = tk // LANES
    q = q_ref[0].reshape(rows, dk)
    assert dv == LANES, "alpha (lane-replicated) is applied to the accumulator without a broadcast"

    s_bufs = (s_a, s_b)

    def chunk(c):
        return pl.ds(c * tk, tk)

    def scores(c):
        return _dot_nt(q, k_ref[0, 0, chunk(c), :])

    def step(c):
        s_cur, s_nxt = s_bufs[c % 2], s_bufs[(c + 1) % 2]
        if c + 1 < nchunks:
            s_nxt[...] = scores(c + 1)
        cols = [s_cur[:, j * LANES:(j + 1) * LANES] for j in range(ncol)]
        mc = functools.reduce(jnp.maximum, cols)
        m_prev = m_s[...]
        m_new = jnp.maximum(m_prev, jnp.max(mc, axis=-1, keepdims=True))
        alpha = jnp.exp2(m_prev - m_new)
        ps = [jnp.exp2(cj - m_new) for cj in cols]
        l_s[...] = alpha * l_s[...] + functools.reduce(jnp.add, ps)
        p = jnp.concatenate([pj.astype(BF16) for pj in ps], axis=-1)
        acc_s[...] = alpha * acc_s[...] + _dot(p, v_ref[0, 0, chunk(c), :])
        m_s[...] = m_new

    m_s[...] = jnp.full(m_s.shape, NEG_BIG, F32)
    l_s[...] = jnp.zeros(l_s.shape, F32)
    acc_s[...] = jnp.zeros(acc_s.shape, F32)
    s_bufs[0][...] = scores(0)
    for c in range(nchunks):
        step(c)
    o = acc_s[...] / jnp.sum(l_s[...], axis=-1, keepdims=True)
    for hh in range(nh):
        o_ref[0, :, hh * dv:(hh + 1) * dv] = o[hh * tq:(hh + 1) * tq].astype(BF16)


def _attention(q, k, v, tq, tk):
    B, H, L, dk = q.shape
    G, S, dv = k.shape[1], k.shape[2], v.shape[3]
    nh = H // G
    return pl.pallas_call(
        functools.partial(_attn_kernel, tk=tk),
        grid=(B, G, L // tq),
        in_specs=[
            pl.BlockSpec((1, nh, tq, dk), lambda b, g, i: (b, g, i, 0)),
            pl.BlockSpec((1, 1, S, dk), lambda b, g, i: (b, g, 0, 0)),
            pl.BlockSpec((1, 1, S, dv), lambda b, g, i: (b, g, 0, 0)),
        ],
        out_specs=pl.BlockSpec((1, tq, nh * dv), lambda b, g, i: (b, i, g)),
        out_shape=jax.ShapeDtypeStruct((B, L, H * dv), BF16),
        scratch_shapes=[
            pltpu.VMEM((nh * tq, tk), F32),
            pltpu.VMEM((nh * tq, tk), F32),
            pltpu.VMEM((nh * tq, LANES), F32),
            pltpu.VMEM((nh * tq, LANES), F32),
            pltpu.VMEM((nh * tq, dv), F32),
        ],
        compiler_params=_params(("parallel", "parallel", "arbitrary")), name="attn",
    )(q, k, v)


def _mix_kernel(x_ref, g_ref, ya_ref, yb_ref, woa_ref, wob_ref, wout_ref, ncross_ref, wxq_ref, kvx_ref,
                wxo_ref, nffn_ref, x2_ref, hf_ref):
    D = x_ref.shape[2]
    a = _dot(ya_ref[0], woa_ref[...])
    b = _dot(yb_ref[0], wob_ref[...])
    m = g_ref[:, 0:D].astype(F32) * a + g_ref[:, D:2 * D].astype(F32) * b
    x1 = x_ref[0] + _dot(m.astype(BF16), wout_ref[...])

    hc = _rms(x1, ncross_ref[...]).astype(BF16)
    q = _dot(hc, wxq_ref[...]) * (X_HEAD_DIM ** -0.5)
    kw = X_HEADS * X_HEAD_DIM
    outs = []
    for hh in range(X_HEADS):
        sl = slice(X_HEAD_DIM * hh, X_HEAD_DIM * (hh + 1))
        s = _dot_nt(q[:, sl].astype(BF16), kvx_ref[0, :, sl])
        p = jnp.exp(s - jnp.max(s, axis=-1, keepdims=True))
        l = jnp.sum(p, axis=-1, keepdims=True)
        o = _dot(p.astype(BF16), kvx_ref[0, :, kw + X_HEAD_DIM * hh:kw + X_HEAD_DIM * (hh + 1)]) / l
        outs.append(o.astype(BF16))
    x2 = x1 + _dot(jnp.concatenate(outs, axis=-1), wxo_ref[...])
    x2_ref[0] = x2
    hf_ref[0] = _rms(x2, nffn_ref[...]).astype(BF16)


def _mix(x, g2d, ya, yb, woa, wob, wout, ncross, wxq, kvx, wxo, nffn, tm):
    B, L, D = x.shape
    nt = L // tm
    tok = lambda b, i: (b, i, 0)
    return pl.pallas_call(
        _mix_kernel, grid=(B, nt),
        in_specs=[
            pl.BlockSpec((1, tm, D), tok),
            pl.BlockSpec((tm, 2 * D), lambda b, i: (b * nt + i, 0)),
            pl.BlockSpec((1, tm, ya.shape[2]), tok),
            pl.BlockSpec((1, tm, yb.shape[2]), tok),
            _resident(woa.shape), _resident(wob.shape), _resident(wout.shape), _resident(ncross.shape),
            _resident(wxq.shape),
            pl.BlockSpec((1,) + kvx.shape[1:], lambda b, i: (b, 0, 0)),
            _resident(wxo.shape), _resident(nffn.shape),
        ],
        out_specs=(pl.BlockSpec((1, tm, D), tok), pl.BlockSpec((1, tm, D), tok)),
        out_shape=(jax.ShapeDtypeStruct((B, L, D), F32), jax.ShapeDtypeStruct((B, L, D), BF16)),
        compiler_params=_params(("parallel", "parallel")), name="mix",
    )(x, g2d, ya, yb, woa, wob, wout, ncross, wxq, kvx, wxo, nffn)


def _ffn_kernel(hm_ref, hp_ref, hn_ref, x2_ref, wa_ref, wb_ref, cwa_ref, cwb_ref, wd_ref, nfin_ref,
                o_ref, hbuf, ua, ub, *, tm):
    i, f = pl.program_id(1), pl.program_id(2)
    halo = BF16_ROWS

    @pl.when(f == 0)
    def _():
        prev, nxt = hp_ref[0], hn_ref[0]
        hbuf[0:halo] = jnp.where(i == 0, jnp.zeros_like(prev), prev)
        hbuf[halo:halo + tm] = hm_ref[0]
        hbuf[halo + tm:] = jnp.where(i == pl.num_programs(1) - 1, jnp.zeros_like(nxt), nxt)
        o_ref[...] = jnp.zeros_like(o_ref)

    hb = hbuf[...]

    def conv(u_ref, cw):
        return (u_ref[halo - 1:halo - 1 + tm] * cw[0:1] + u_ref[halo:halo + tm] * cw[1:2]
                + u_ref[halo + 1:halo + 1 + tm] * cw[2:3] + cw[3:4])

    ua[...] = _dot(hb, wa_ref[...])
    ub[...] = _dot(hb, wb_ref[...])
    a = conv(ua, cwa_ref[...])
    b = conv(ub, cwb_ref[...])
    act = a * (1.0 / (1.0 + jnp.exp(-a))) * b
    o_ref[0] += _dot(act.astype(BF16), wd_ref[...])

    @pl.when(f == pl.num_programs(2) - 1)
    def _():
        o_ref[0] = _rms(x2_ref[0] + o_ref[0], nfin_ref[...])


def _ffn(hf, x2, wup, cw, wdown, nfin, tm, tf):
    B, L, D = hf.shape
    dff = wdown.shape[0]
    nf = dff // tf
    hb = tm // BF16_ROWS
    last = L // BF16_ROWS - 1
    tok = lambda b, i, f: (b, i, 0)
    return pl.pallas_call(
        functools.partial(_ffn_kernel, tm=tm),
        grid=(B, L // tm, nf),
        in_specs=[
            pl.BlockSpec((1, tm, D), tok),
            pl.BlockSpec((1, BF16_ROWS, D), lambda b, i, f: (b, jnp.maximum(i * hb - 1, 0), 0)),
            pl.BlockSpec((1, BF16_ROWS, D), lambda b, i, f: (b, jnp.minimum((i + 1) * hb, last), 0)),
            pl.BlockSpec((1, tm, D), tok),
            pl.BlockSpec((D, tf), lambda b, i, f: (0, f)),
            pl.BlockSpec((D, tf), lambda b, i, f: (0, f + nf)),
            pl.BlockSpec((8, tf), lambda b, i, f: (0, f)),
            pl.BlockSpec((8, tf), lambda b, i, f: (0, f + nf)),
            pl.BlockSpec((tf, D), lambda b, i, f: (f, 0)),
            pl.BlockSpec(nfin.shape, lambda b, i, f: (0, 0)),
        ],
        out_specs=pl.BlockSpec((1, tm, D), tok),
        out_shape=jax.ShapeDtypeStruct((B, L, D), F32),
        scratch_shapes=[pltpu.VMEM((tm + 2 * BF16_ROWS, D), BF16),
                        pltpu.VMEM((tm + 2 * BF16_ROWS, tf), F32), pltpu.VMEM((tm + 2 * BF16_ROWS, tf), F32)],
        compiler_params=_params(("parallel", "parallel", "arbitrary")), name="ffn",
    )(hf, hf, hf, x2, wup, wup, cw, cw, wdown, nfin)


def _rope_tables(L):
    half = MLA_ROPE // 2
    inv = jnp.power(ROPE_THETA, -2.0 * jnp.arange(half, dtype=F32) / (2 * half))
    t = jnp.arange(L, dtype=jnp.int32)

    def cs(pos):
        ang = pos.astype(F32)[:, None] * inv[None, :]
        return jnp.cos(ang), jnp.sin(ang)

    z = jnp.zeros((L, half), F32)
    (ct, st), (cr, sr), (cc, sc) = lax.optimization_barrier((cs(t), cs(t // GRID_W), cs(t % GRID_W)))
    cat = lambda *p: jnp.concatenate(p, axis=-1)
    return jnp.stack([
        cat(ct, ct, z, z), cat(-st, z, z, z), cat(z, st, z, z),
        cat(cr, cr, cc, cc), cat(-sr, z, -sc, z), cat(z, sr, z, sc),
    ])


def kernel(x, mem, norm_mix, w_in, mla_q_norm, w_uq, mla_kv_norm, w_ukv, gqa_q_norm, gqa_k_norm, w_o_mla,
           w_o_gqa, w_gate, b_gate, w_out, norm_cross, norm_mem, w_xq, w_xkv, w_xo, norm_ffn, w_up, conv_w,
           conv_b, w_down, norm_final):
    B, L, D = x.shape
    depth = w_in.shape[0]
    assert depth == 1, "the final norm is fused into the (single) layer's FFN kernel"
    tabs = _rope_tables(L)
    row = lambda v: v.reshape(1, -1).astype(F32)
    tm_prep = min(256, L)
    tm_gate = min(1024, L)
    tm_mix = min(256, L)
    tm_ffn = min(512, L)
    tq_a = min(512, L)
    tq_b = min(128, L)
    tk = min(512, L)

    for l in range(depth):
        wi = w_in[l]
        o_kr = MLA_Q_RANK + MLA_KV_RANK
        o_qb = o_kr + MLA_ROPE
        assert o_qb + (GQA_HEADS + 2 * GQA_KV_HEADS) * GQA_HEAD_DIM == wi.shape[1]
        wlat = wi[:, :o_kr].astype(BF16)
        wgqa = wi[:, o_qb:].astype(BF16)
        wkr = jnp.pad(wi[:, o_kr:o_qb], ((0, 0), (0, LANES - MLA_ROPE))).astype(BF16)
        wuq = jnp.pad(w_uq[l].reshape(MLA_Q_RANK, MLA_HEADS, MLA_NOPE + MLA_ROPE),
                      ((0, 0), (0, 0), (0, 2 * LANES - MLA_NOPE - MLA_ROPE))
                      ).reshape(MLA_Q_RANK, MLA_HEADS * 2 * LANES).astype(BF16)
        wkv = w_ukv[l].reshape(MLA_KV_RANK, MLA_HEADS, MLA_NOPE + MLA_V)
        wukv = jnp.concatenate([wkv[:, :, :MLA_NOPE].reshape(MLA_KV_RANK, -1),
                                wkv[:, :, MLA_NOPE:].reshape(MLA_KV_RANK, -1)], axis=1).astype(BF16)
        wx = w_xkv[l].reshape(D, X_HEADS, 2 * X_HEAD_DIM)
        wxkv = jnp.concatenate([wx[:, :, :X_HEAD_DIM].reshape(D, -1),
                                wx[:, :, X_HEAD_DIM:].reshape(D, -1)], axis=1).astype(BF16)
        cw = jnp.concatenate([conv_w[l], conv_b[l][None, :],
                              jnp.zeros((8 - CONV_W - 1, conv_w.shape[2]), F32)], axis=0)

        h, qm, km, vm, qg, kg, vg = _prep(
            x, row(norm_mix[l]), wlat, wgqa, wkr, row(mla_q_norm[l]), wuq, row(mla_kv_norm[l]), wukv,
            row(gqa_q_norm[l]), row(gqa_k_norm[l]), tabs, tm_prep)
        g = _gate(h.reshape(B * L, D), w_gate[l].astype(BF16), row(b_gate[l]), tm_gate)
        ya = _attention(qm, km, vm, tq_a, tk)
        yb = _attention(qg, kg, vg, tq_b, tk)
        kvx = _memkv(mem, row(norm_mem[l]), wxkv)
        x2, hf = _mix(x, g, ya, yb, w_o_mla[l].astype(BF16), w_o_gqa[l].astype(BF16), w_out[l].astype(BF16),
                      row(norm_cross[l]), w_xq[l].astype(BF16), kvx, w_xo[l].astype(BF16), row(norm_ffn[l]),
                      tm_mix)
        x = _ffn(hf, x2, w_up[l].astype(BF16), cw, w_down[l].astype(BF16), row(norm_final), tm_ffn,
                 min(512, w_down.shape[1]))
    return x
```

```python
import functools

import jax
import jax.numpy as jnp
import numpy as np
from jax import lax
from jax.experimental import pallas as pl
from jax.experimental.pallas import tpu as pltpu

F32 = jnp.float32
BF16 = jnp.bfloat16

EPS = 1e-6
ROPE_THETA = 10000.0
GRID_W = 64
MLA_HEADS = 8
MLA_Q_RANK = 512
MLA_KV_RANK = 512
MLA_NOPE = 128
MLA_ROPE = 64
MLA_V = 128
GQA_HEADS = 8
GQA_KV_HEADS = 2
GQA_HEAD_DIM = 128
X_HEADS = 4
X_HEAD_DIM = 128
CONV_W = 3

LANES = 128
BF16_ROWS = 16
VMEM_LIMIT = 56 * 1024 * 1024
NEG_BIG = -1e30
LOG2E = float(np.log2(np.e))


def _params(semantics):
    return pltpu.CompilerParams(dimension_semantics=semantics, vmem_limit_bytes=VMEM_LIMIT)


def _resident(shape):
    zeros = (0,) * len(shape)
    return pl.BlockSpec(shape, lambda *_: zeros, pipeline_mode=pl.Buffered(1))


def _rms(x, gain):
    ms = jnp.mean(x * x, axis=-1, keepdims=True)
    return x * lax.rsqrt(ms + EPS) * gain


def _dot(a, b):
    return jnp.dot(a, b, preferred_element_type=F32)


def _dot_nt(a, b):
    return lax.dot_general(a, b, (((1,), (1,)), ((), ())), preferred_element_type=F32)


def _rope(t, c, sa, sb):
    return t * c + pltpu.roll(t, LANES - 32, 1) * sa + pltpu.roll(t, 32, 1) * sb


def _prep_kernel(x_ref, nmix_ref, wlat_ref, wgqa_ref, wkr_ref, qn_ref, wuq_ref, kvn_ref, wukv_ref, gq_ref,
                 gk_ref, tab_ref, h_ref, qm_ref, km_ref, vm_ref, qg_ref, kg_ref, vg_ref):
    x = x_ref[0]
    h = _rms(x, nmix_ref[...]).astype(BF16)
    h_ref[0] = h
    zl = _dot(h, wlat_ref[...])
    zg = _dot(h, wgqa_ref[...])
    zr = _dot(h, wkr_ref[...])
    c1, sa1, sb1 = tab_ref[0], tab_ref[1], tab_ref[2]
    c2, sa2, sb2 = tab_ref[3], tab_ref[4], tab_ref[5]

    scale_a = (MLA_NOPE + MLA_ROPE) ** -0.5 * LOG2E
    cq = _rms(zl[:, 0:MLA_Q_RANK], qn_ref[...]).astype(BF16)
    q = _dot(cq, wuq_ref[...])
    for hh in range(MLA_HEADS):
        base = 2 * LANES * hh
        qm_ref[0, hh, :, 0:LANES] = (q[:, base:base + LANES] * scale_a).astype(BF16)
        r = _rope(q[:, base + LANES:base + 2 * LANES], c1, sa1, sb1)
        qm_ref[0, hh, :, LANES:2 * LANES] = (r * scale_a).astype(BF16)

    ckv = _rms(zl[:, MLA_Q_RANK:MLA_Q_RANK + MLA_KV_RANK], kvn_ref[...]).astype(BF16)
    kv = _dot(ckv, wukv_ref[...])
    kr = _rope(zr, c1, sa1, sb1).astype(BF16)
    for hh in range(MLA_HEADS):
        km_ref[0, hh, :, 0:LANES] = kv[:, LANES * hh:LANES * (hh + 1)].astype(BF16)
        km_ref[0, hh, :, LANES:2 * LANES] = kr
        vb = MLA_HEADS * MLA_NOPE + MLA_V * hh
        vm_ref[0, hh] = kv[:, vb:vb + MLA_V].astype(BF16)

    scale_b = GQA_HEAD_DIM ** -0.5 * LOG2E
    dh = GQA_HEAD_DIM
    o_k = GQA_HEADS * dh
    o_v = o_k + GQA_KV_HEADS * dh
    for hh in range(GQA_HEADS):
        t = _rms(zg[:, dh * hh:dh * (hh + 1)], gq_ref[...])
        qg_ref[0, hh] = (_rope(t, c2, sa2, sb2) * scale_b).astype(BF16)
    for hh in range(GQA_KV_HEADS):
        t = _rms(zg[:, o_k + dh * hh:o_k + dh * (hh + 1)], gk_ref[...])
        kg_ref[0, hh] = _rope(t, c2, sa2, sb2).astype(BF16)
        vg_ref[0, hh] = zg[:, o_v + dh * hh:o_v + dh * (hh + 1)].astype(BF16)


def _prep(x, nmix, wlat, wgqa, wkr, qn, wuq, kvn, wukv, gq, gk, tabs, tm):
    B, L, D = x.shape
    grid = (B, L // tm)
    tok = lambda b, i: (b, i, 0)
    head = lambda b, i: (b, 0, i, 0)
    out_shape = (
        jax.ShapeDtypeStruct((B, L, D), BF16),
        jax.ShapeDtypeStruct((B, MLA_HEADS, L, 2 * LANES), BF16),
        jax.ShapeDtypeStruct((B, MLA_HEADS, L, 2 * LANES), BF16),
        jax.ShapeDtypeStruct((B, MLA_HEADS, L, MLA_V), BF16),
        jax.ShapeDtypeStruct((B, GQA_HEADS, L, GQA_HEAD_DIM), BF16),
        jax.ShapeDtypeStruct((B, GQA_KV_HEADS, L, GQA_HEAD_DIM), BF16),
        jax.ShapeDtypeStruct((B, GQA_KV_HEADS, L, GQA_HEAD_DIM), BF16),
    )
    out_specs = (
        pl.BlockSpec((1, tm, D), tok),
        pl.BlockSpec((1, MLA_HEADS, tm, 2 * LANES), head),
        pl.BlockSpec((1, MLA_HEADS, tm, 2 * LANES), head),
        pl.BlockSpec((1, MLA_HEADS, tm, MLA_V), head),
        pl.BlockSpec((1, GQA_HEADS, tm, GQA_HEAD_DIM), head),
        pl.BlockSpec((1, GQA_KV_HEADS, tm, GQA_HEAD_DIM), head),
        pl.BlockSpec((1, GQA_KV_HEADS, tm, GQA_HEAD_DIM), head),
    )
    in_specs = [
        pl.BlockSpec((1, tm, D), tok),
        _resident(nmix.shape), _resident(wlat.shape), _resident(wgqa.shape), _resident(wkr.shape),
        _resident(qn.shape), _resident(wuq.shape),
        _resident(kvn.shape), _resident(wukv.shape), _resident(gq.shape), _resident(gk.shape),
        pl.BlockSpec((6, tm, LANES), lambda b, i: (0, i, 0)),
    ]
    return pl.pallas_call(
        _prep_kernel, grid=grid, in_specs=in_specs, out_specs=out_specs, out_shape=out_shape,
        compiler_params=_params(("parallel", "parallel")), name="prep",
    )(x, nmix, wlat, wgqa, wkr, qn, wuq, kvn, wukv, gq, gk, tabs)


def _gate_kernel(h_ref, w_ref, b_ref, g_ref, *, chunk):
    h = h_ref[...]
    for c in range(w_ref.shape[1] // chunk):
        sl = slice(c * chunk, (c + 1) * chunk)
        pre = _dot(h, w_ref[:, sl]) + b_ref[:, sl]
        g_ref[:, sl] = (1.0 / (1.0 + jnp.exp(-pre))).astype(BF16)


def _gate(h2d, wg, bg, tm):
    T, D = h2d.shape
    N = wg.shape[1]
    return pl.pallas_call(
        functools.partial(_gate_kernel, chunk=1024),
        grid=(T // tm,),
        in_specs=[pl.BlockSpec((tm, D), lambda i: (i, 0)), _resident(wg.shape), _resident(bg.shape)],
        out_specs=pl.BlockSpec((tm, N), lambda i: (i, 0)),
        out_shape=jax.ShapeDtypeStruct((T, N), BF16),
        compiler_params=_params(("parallel",)), name="gate",
    )(h2d, wg, bg)


def _memkv_kernel(mem_ref, nmem_ref, w_ref, o_ref):
    mn = _rms(mem_ref[0], nmem_ref[...]).astype(BF16)
    o_ref[0] = _dot(mn, w_ref[...]).astype(BF16)


def _memkv(mem, nmem, wxkv):
    B, M, D = mem.shape
    N = wxkv.shape[1]
    return pl.pallas_call(
        _memkv_kernel, grid=(B,),
        in_specs=[pl.BlockSpec((1, M, D), lambda b: (b, 0, 0)), _resident(nmem.shape), _resident(wxkv.shape)],
        out_specs=pl.BlockSpec((1, M, N), lambda b: (b, 0, 0)),
        out_shape=jax.ShapeDtypeStruct((B, M, N), BF16),
        compiler_params=_params(("parallel",)), name="memkv",
    )(mem, nmem, wxkv)


def _attn_kernel(q_ref, k_ref, v_ref, o_ref, vx, s_a, s_b, m_s, acc_s, *, tk, nsub):
    nh, dk = q_ref.shape[1], q_ref.shape[3]
    tq = q_ref.shape[2] // nsub
    S, dv = v_ref.shape[2], v_ref.shape[3]
    rows = nh * tq
    nchunks = S // tk
    ncol = tk // LANES
    assert dv == LANES, "alpha (lane-replicated) is applied to the accumulator without a broadcast"

    @pl.when(pl.program_id(2) == 0)
    def _():
        vx[:, 0:dv] = v_ref[0, 0]
        vx[:, dv:2 * dv] = jnp.ones((S, dv), BF16)

    def chunk(c):
        return pl.ds(c * tk, tk)

    for u in range(nsub):
        q = q_ref[0, :, u * tq:(u + 1) * tq, :].reshape(rows, dk)
        s_bufs = (s_a.at[u], s_b.at[u])
        m_u, acc_u = m_s.at[u], acc_s.at[u]

        def scores(c):
            return _dot_nt(q, k_ref[0, 0, chunk(c), :])

        def step(c):
            s_cur, s_nxt = s_bufs[c % 2], s_bufs[(c + 1) % 2]
            if c + 1 < nchunks:
                s_nxt[...] = scores(c + 1)
            cols = [s_cur[:, j * LANES:(j + 1) * LANES] for j in range(ncol)]
            mc = functools.reduce(jnp.maximum, cols)
            m_prev = m_u[...]
            m_new = jnp.maximum(m_prev, jnp.max(mc, axis=-1, keepdims=True))
            alpha = jnp.exp2(m_prev - m_new)
            p = jnp.concatenate([jnp.exp2((cj - m_new).astype(BF16)) for cj in cols], axis=-1)
            acc_u[...] = jnp.concatenate([alpha, alpha], axis=-1) * acc_u[...] + _dot(p, vx[chunk(c), :])
            m_u[...] = m_new

        m_u[...] = jnp.full(m_u.shape, NEG_BIG, F32)
        acc_u[...] = jnp.zeros(acc_u.shape, F32)
        s_bufs[0][...] = scores(0)
        for c in range(nchunks):
            step(c)
        o = acc_u[:, 0:dv] / acc_u[:, dv:2 * dv]
        for hh in range(nh):
            o_ref[0, u * tq:(u + 1) * tq, hh * dv:(hh + 1) * dv] = o[hh * tq:(hh + 1) * tq].astype(BF16)


def _attention(q, k, v, tq, tk, nsub):
    B, H, L, dk = q.shape
    G, S, dv = k.shape[1], k.shape[2], v.shape[3]
    nh = H // G
    tb = nsub * tq
    return pl.pallas_call(
        functools.partial(_attn_kernel, tk=tk, nsub=nsub),
        grid=(B, G, L // tb),
        in_specs=[
            pl.BlockSpec((1, nh, tb, dk), lambda b, g, i: (b, g, i, 0)),
            pl.BlockSpec((1, 1, S, dk), lambda b, g, i: (b, g, 0, 0)),
            pl.BlockSpec((1, 1, S, dv), lambda b, g, i: (b, g, 0, 0)),
        ],
        out_specs=pl.BlockSpec((1, tb, nh * dv), lambda b, g, i: (b, i, g)),
        out_shape=jax.ShapeDtypeStruct((B, L, H * dv), BF16),
        scratch_shapes=[
            pltpu.VMEM((S, 2 * dv), BF16),
            pltpu.VMEM((nsub, nh * tq, tk), F32),
            pltpu.VMEM((nsub, nh * tq, tk), F32),
            pltpu.VMEM((nsub, nh * tq, LANES), F32),
            pltpu.VMEM((nsub, nh * tq, 2 * dv), F32),
        ],
        compiler_params=_params(("parallel", "parallel", "arbitrary")), name="attn",
    )(q, k, v)


def _mix_kernel(x_ref, g_ref, ya_ref, yb_ref, woa_ref, wob_ref, wout_ref, ncross_ref, wxq_ref, kvx_ref,
                wxo_ref, nffn_ref, x2_ref, hf_ref):
    D = x_ref.shape[2]
    a = _dot(ya_ref[0], woa_ref[...])
    b = _dot(yb_ref[0], wob_ref[...])
    m = g_ref[:, 0:D].astype(F32) * a + g_ref[:, D:2 * D].astype(F32) * b
    x1 = x_ref[0] + _dot(m.astype(BF16), wout_ref[...])

    hc = _rms(x1, ncross_ref[...]).astype(BF16)
    q = _dot(hc, wxq_ref[...]) * (X_HEAD_DIM ** -0.5)
    kw = X_HEADS * X_HEAD_DIM
    outs = []
    for hh in range(X_HEADS):
        sl = slice(X_HEAD_DIM * hh, X_HEAD_DIM * (hh + 1))
        s = _dot_nt(q[:, sl].astype(BF16), kvx_ref[0, :, sl])
        p = jnp.exp(s - jnp.max(s, axis=-1, keepdims=True))
        l = jnp.sum(p, axis=-1, keepdims=True)
        o = _dot(p.astype(BF16), kvx_ref[0, :, kw + X_HEAD_DIM * hh:kw + X_HEAD_DIM * (hh + 1)]) / l
        outs.append(o.astype(BF16))
    x2 = x1 + _dot(jnp.concatenate(outs, axis=-1), wxo_ref[...])
    x2_ref[0] = x2
    hf_ref[0] = _rms(x2, nffn_ref[...]).astype(BF16)


def _mix(x, g2d, ya, yb, woa, wob, wout, ncross, wxq, kvx, wxo, nffn, tm):
    B, L, D = x.shape
    nt = L // tm
    tok = lambda b, i: (b, i, 0)
    return pl.pallas_call(
        _mix_kernel, grid=(B, nt),
        in_specs=[
            pl.BlockSpec((1, tm, D), tok),
            pl.BlockSpec((tm, 2 * D), lambda b, i: (b * nt + i, 0)),
            pl.BlockSpec((1, tm, ya.shape[2]), tok),
            pl.BlockSpec((1, tm, yb.shape[2]), tok),
            _resident(woa.shape), _resident(wob.shape), _resident(wout.shape), _resident(ncross.shape),
            _resident(wxq.shape),
            pl.BlockSpec((1,) + kvx.shape[1:], lambda b, i: (b, 0, 0)),
            _resident(wxo.shape), _resident(nffn.shape),
        ],
        out_specs=(pl.BlockSpec((1, tm, D), tok), pl.BlockSpec((1, tm, D), tok)),
        out_shape=(jax.ShapeDtypeStruct((B, L, D), F32), jax.ShapeDtypeStruct((B, L, D), BF16)),
        compiler_params=_params(("parallel", "parallel")), name="mix",
    )(x, g2d, ya, yb, woa, wob, wout, ncross, wxq, kvx, wxo, nffn)


def _ffn_kernel(hm_ref, hp_ref, hn_ref, x2_ref, wa_ref, wb_ref, cwa_ref, cwb_ref, wd_ref, nfin_ref,
                o_ref, hbuf, ua, ub, *, tm):
    i, f = pl.program_id(1), pl.program_id(2)
    halo = BF16_ROWS

    @pl.when(f == 0)
    def _():
        prev, nxt = hp_ref[0], hn_ref[0]
        hbuf[0:halo] = jnp.where(i == 0, jnp.zeros_like(prev), prev)
        hbuf[halo:halo + tm] = hm_ref[0]
        hbuf[halo + tm:] = jnp.where(i == pl.num_programs(1) - 1, jnp.zeros_like(nxt), nxt)
        o_ref[...] = jnp.zeros_like(o_ref)

    hb = hbuf[...]

    def conv(u_ref, cw):
        return (u_ref[halo - 1:halo - 1 + tm] * cw[0:1] + u_ref[halo:halo + tm] * cw[1:2]
                + u_ref[halo + 1:halo + 1 + tm] * cw[2:3] + cw[3:4])

    ua[...] = _dot(hb, wa_ref[...])
    ub[...] = _dot(hb, wb_ref[...])
    a = conv(ua, cwa_ref[...])
    b = conv(ub, cwb_ref[...])
    act = a * (1.0 / (1.0 + jnp.exp(-a))) * b
    o_ref[0] += _dot(act.astype(BF16), wd_ref[...])

    @pl.when(f == pl.num_programs(2) - 1)
    def _():
        o_ref[0] = _rms(x2_ref[0] + o_ref[0], nfin_ref[...])


def _ffn(hf, x2, wup, cw, wdown, nfin, tm, tf):
    B, L, D = hf.shape
    dff = wdown.shape[0]
    nf = dff // tf
    hb = tm // BF16_ROWS
    last = L // BF16_ROWS - 1
    tok = lambda b, i, f: (b, i, 0)
    return pl.pallas_call(
        functools.partial(_ffn_kernel, tm=tm),
        grid=(B, L // tm, nf),
        in_specs=[
            pl.BlockSpec((1, tm, D), tok),
            pl.BlockSpec((1, BF16_ROWS, D), lambda b, i, f: (b, jnp.maximum(i * hb - 1, 0), 0)),
            pl.BlockSpec((1, BF16_ROWS, D), lambda b, i, f: (b, jnp.minimum((i + 1) * hb, last), 0)),
            pl.BlockSpec((1, tm, D), tok),
            pl.BlockSpec((D, tf), lambda b, i, f: (0, f)),
            pl.BlockSpec((D, tf), lambda b, i, f: (0, f + nf)),
            pl.BlockSpec((8, tf), lambda b, i, f: (0, f)),
            pl.BlockSpec((8, tf), lambda b, i, f: (0, f + nf)),
            pl.BlockSpec((tf, D), lambda b, i, f: (f, 0)),
            pl.BlockSpec(nfin.shape, lambda b, i, f: (0, 0)),
        ],
        out_specs=pl.BlockSpec((1, tm, D), tok),
        out_shape=jax.ShapeDtypeStruct((B, L, D), F32),
        scratch_shapes=[pltpu.VMEM((tm + 2 * BF16_ROWS, D), BF16),
                        pltpu.VMEM((tm + 2 * BF16_ROWS, tf), F32), pltpu.VMEM((tm + 2 * BF16_ROWS, tf), F32)],
        compiler_params=_params(("parallel", "parallel", "arbitrary")), name="ffn",
    )(hf, hf, hf, x2, wup, wup, cw, cw, wdown, nfin)


def _rope_tables(L):
    half = MLA_ROPE // 2
    inv = np.power(np.float32(ROPE_THETA), (-2.0 * np.arange(half, dtype=np.float32) / (2 * half)).astype(np.float32))
    t = np.arange(L)

    def cs(pos):
        ang = (pos.astype(np.float32)[:, None] * inv[None, :]).astype(np.float32)
        return np.cos(ang).astype(np.float32), np.sin(ang).astype(np.float32)

    z = np.zeros((L, half), np.float32)
    (ct, st), (cr, sr), (cc, sc) = cs(t), cs(t // GRID_W), cs(t % GRID_W)
    cat = lambda *p: np.concatenate(p, axis=-1)
    return jnp.asarray(np.stack([
        cat(ct, ct, z, z), cat(-st, z, z, z), cat(z, st, z, z),
        cat(cr, cr, cc, cc), cat(-sr, z, -sc, z), cat(z, sr, z, sc),
    ]))


def kernel(x, mem, norm_mix, w_in, mla_q_norm, w_uq, mla_kv_norm, w_ukv, gqa_q_norm, gqa_k_norm, w_o_mla,
           w_o_gqa, w_gate, b_gate, w_out, norm_cross, norm_mem, w_xq, w_xkv, w_xo, norm_ffn, w_up, conv_w,
           conv_b, w_down, norm_final):
    B, L, D = x.shape
    depth = w_in.shape[0]
    assert depth == 1, "the final norm is fused into the (single) layer's FFN kernel"
    tabs = _rope_tables(L)
    row = lambda v: v.reshape(1, -1).astype(F32)
    tm_prep = min(256, L)
    tm_gate = min(1024, L)
    tm_mix = min(256, L)
    tm_ffn = min(512, L)
    tq_a = min(512, L)
    tq_b = min(128, L)
    tk = min(512, L)
    nsub = 4 if L >= 4 * tq_a else 1

    for l in range(depth):
        wi = w_in[l]
        o_kr = MLA_Q_RANK + MLA_KV_RANK
        o_qb = o_kr + MLA_ROPE
        assert o_qb + (GQA_HEADS + 2 * GQA_KV_HEADS) * GQA_HEAD_DIM == wi.shape[1]
        wlat = wi[:, :o_kr].astype(BF16)
        wgqa = wi[:, o_qb:].astype(BF16)
        wkr = jnp.pad(wi[:, o_kr:o_qb], ((0, 0), (0, LANES - MLA_ROPE))).astype(BF16)
        wuq = jnp.pad(w_uq[l].reshape(MLA_Q_RANK, MLA_HEADS, MLA_NOPE + MLA_ROPE),
                      ((0, 0), (0, 0), (0, 2 * LANES - MLA_NOPE - MLA_ROPE))
                      ).reshape(MLA_Q_RANK, MLA_HEADS * 2 * LANES).astype(BF16)
        wkv = w_ukv[l].reshape(MLA_KV_RANK, MLA_HEADS, MLA_NOPE + MLA_V)
        wukv = jnp.concatenate([wkv[:, :, :MLA_NOPE].reshape(MLA_KV_RANK, -1),
                                wkv[:, :, MLA_NOPE:].reshape(MLA_KV_RANK, -1)], axis=1).astype(BF16)
        wx = w_xkv[l].reshape(D, X_HEADS, 2 * X_HEAD_DIM)
        wxkv = jnp.concatenate([wx[:, :, :X_HEAD_DIM].reshape(D, -1),
                                wx[:, :, X_HEAD_DIM:].reshape(D, -1)], axis=1).astype(BF16)
        cw = jnp.concatenate([conv_w[l], conv_b[l][None, :],
                              jnp.zeros((8 - CONV_W - 1, conv_w.shape[2]), F32)], axis=0)

        h, qm, km, vm, qg, kg, vg = _prep(
            x, row(norm_mix[l]), wlat, wgqa, wkr, row(mla_q_norm[l]), wuq, row(mla_kv_norm[l]), wukv,
            row(gqa_q_norm[l]), row(gqa_k_norm[l]), tabs, tm_prep)
        g = _gate(h.reshape(B * L, D), w_gate[l].astype(BF16), row(b_gate[l]), tm_gate)
        ya = _attention(qm, km, vm, tq_a, tk, nsub)
        yb = _attention(qg, kg, vg, tq_b, tk, nsub)
        kvx = _memkv(mem, row(norm_mem[l]), wxkv)
        x2, hf = _mix(x, g, ya, yb, w_o_mla[l].astype(BF16), w_o_gqa[l].astype(BF16), w_out[l].astype(BF16),
                      row(norm_cross[l]), w_xq[l].astype(BF16), kvx, w_xo[l].astype(BF16), row(norm_ffn[l]),
                      tm_mix)
        x = _ffn(hf, x2, w_up[l].astype(BF16), cw, w_down[l].astype(BF16), row(norm_final), tm_ffn,
                 min(512, w_down.shape[1]))
    return x
```

```python
import functools

import jax
import jax.numpy as jnp
import numpy as np
from jax import lax
from jax.experimental import pallas as pl
from jax.experimental.pallas import tpu as pltpu

F32 = jnp.float32
BF16 = jnp.bfloat16

EPS = 1e-6
ROPE_THETA = 10000.0
GRID_W = 64
MLA_HEADS = 8
MLA_Q_RANK = 512
MLA_KV_RANK = 512
MLA_NOPE = 128
MLA_ROPE = 64
MLA_V = 128
GQA_HEADS = 8
GQA_KV_HEADS = 2
GQA_HEAD_DIM = 128
X_HEADS = 4
X_HEAD_DIM = 128
CONV_W = 3

LANES = 128
BF16_ROWS = 16
VMEM_LIMIT = 56 * 1024 * 1024
NEG_BIG = -1e30
LOG2E = float(np.log2(np.e))


def _params(semantics):
    return pltpu.CompilerParams(dimension_semantics=semantics, vmem_limit_bytes=VMEM_LIMIT)


def _resident(shape):
    zeros = (0,) * len(shape)
    return pl.BlockSpec(shape, lambda *_: zeros, pipeline_mode=pl.Buffered(1))


def _rms(x, gain):
    ms = jnp.mean(x * x, axis=-1, keepdims=True)
    return x * lax.rsqrt(ms + EPS) * gain


def _dot(a, b):
    return jnp.dot(a, b, preferred_element_type=F32)


def _dot_nt(a, b):
    return lax.dot_general(a, b, (((1,), (1,)), ((), ())), preferred_element_type=F32)


def _rope(t, c, sa, sb):
    return t * c + pltpu.roll(t, LANES - 32, 1) * sa + pltpu.roll(t, 32, 1) * sb


def _prep_kernel(x_ref, nmix_ref, wlat_ref, wgqa_ref, wkr_ref, qn_ref, wuq_ref, kvn_ref, wukv_ref, gq_ref,
                 gk_ref, tab_ref, h_ref, qm_ref, km_ref, vm_ref, qg_ref, kg_ref, vg_ref):
    x = x_ref[0]
    h = _rms(x, nmix_ref[...]).astype(BF16)
    h_ref[0] = h
    zl = _dot(h, wlat_ref[...])
    zg = _dot(h, wgqa_ref[...])
    zr = _dot(h, wkr_ref[...])
    c1, sa1, sb1 = tab_ref[0], tab_ref[1], tab_ref[2]
    c2, sa2, sb2 = tab_ref[3], tab_ref[4], tab_ref[5]

    scale_a = (MLA_NOPE + MLA_ROPE) ** -0.5 * LOG2E
    cq = _rms(zl[:, 0:MLA_Q_RANK], qn_ref[...]).astype(BF16)
    q = _dot(cq, wuq_ref[...])
    for hh in range(MLA_HEADS):
        base = 2 * LANES * hh
        qm_ref[0, hh, :, 0:LANES] = (q[:, base:base + LANES] * scale_a).astype(BF16)
        r = _rope(q[:, base + LANES:base + 2 * LANES], c1, sa1, sb1)
        qm_ref[0, hh, :, LANES:2 * LANES] = (r * scale_a).astype(BF16)

    ckv = _rms(zl[:, MLA_Q_RANK:MLA_Q_RANK + MLA_KV_RANK], kvn_ref[...]).astype(BF16)
    kv = _dot(ckv, wukv_ref[...])
    kr = _rope(zr, c1, sa1, sb1).astype(BF16)
    for hh in range(MLA_HEADS):
        km_ref[0, hh, :, 0:LANES] = kv[:, LANES * hh:LANES * (hh + 1)].astype(BF16)
        km_ref[0, hh, :, LANES:2 * LANES] = kr
        vb = MLA_HEADS * MLA_NOPE + MLA_V * hh
        vm_ref[0, hh] = kv[:, vb:vb + MLA_V].astype(BF16)

    scale_b = GQA_HEAD_DIM ** -0.5 * LOG2E
    dh = GQA_HEAD_DIM
    o_k = GQA_HEADS * dh
    o_v = o_k + GQA_KV_HEADS * dh
    for hh in range(GQA_HEADS):
        t = _rms(zg[:, dh * hh:dh * (hh + 1)], gq_ref[...])
        qg_ref[0, hh] = (_rope(t, c2, sa2, sb2) * scale_b).astype(BF16)
    for hh in range(GQA_KV_HEADS):
        t = _rms(zg[:, o_k + dh * hh:o_k + dh * (hh + 1)], gk_ref[...])
        kg_ref[0, hh] = _rope(t, c2, sa2, sb2).astype(BF16)
        vg_ref[0, hh] = zg[:, o_v + dh * hh:o_v + dh * (hh + 1)].astype(BF16)


def _prep(x, nmix, wlat, wgqa, wkr, qn, wuq, kvn, wukv, gq, gk, tabs, tm):
    B, L, D = x.shape
    grid = (B, L // tm)
    tok = lambda b, i: (b, i, 0)
    head = lambda b, i: (b, 0, i, 0)
    out_shape = (
        jax.ShapeDtypeStruct((B, L, D), BF16),
        jax.ShapeDtypeStruct((B, MLA_HEADS, L, 2 * LANES), BF16),
        jax.ShapeDtypeStruct((B, MLA_HEADS, L, 2 * LANES), BF16),
        jax.ShapeDtypeStruct((B, MLA_HEADS, L, MLA_V), BF16),
        jax.ShapeDtypeStruct((B, GQA_HEADS, L, GQA_HEAD_DIM), BF16),
        jax.ShapeDtypeStruct((B, GQA_KV_HEADS, L, GQA_HEAD_DIM), BF16),
        jax.ShapeDtypeStruct((B, GQA_KV_HEADS, L, GQA_HEAD_DIM), BF16),
    )
    out_specs = (
        pl.BlockSpec((1, tm, D), tok),
        pl.BlockSpec((1, MLA_HEADS, tm, 2 * LANES), head),
        pl.BlockSpec((1, MLA_HEADS, tm, 2 * LANES), head),
        pl.BlockSpec((1, MLA_HEADS, tm, MLA_V), head),
        pl.BlockSpec((1, GQA_HEADS, tm, GQA_HEAD_DIM), head),
        pl.BlockSpec((1, GQA_KV_HEADS, tm, GQA_HEAD_DIM), head),
        pl.BlockSpec((1, GQA_KV_HEADS, tm, GQA_HEAD_DIM), head),
    )
    in_specs = [
        pl.BlockSpec((1, tm, D), tok),
        _resident(nmix.shape), _resident(wlat.shape), _resident(wgqa.shape), _resident(wkr.shape),
        _resident(qn.shape), _resident(wuq.shape),
        _resident(kvn.shape), _resident(wukv.shape), _resident(gq.shape), _resident(gk.shape),
        pl.BlockSpec((6, tm, LANES), lambda b, i: (0, i, 0)),
    ]
    return pl.pallas_call(
        _prep_kernel, grid=grid, in_specs=in_specs, out_specs=out_specs, out_shape=out_shape,
        compiler_params=_params(("parallel", "parallel")), name="prep",
    )(x, nmix, wlat, wgqa, wkr, qn, wuq, kvn, wukv, gq, gk, tabs)


def _gate_kernel(h_ref, w_ref, b_ref, g_ref, *, chunk):
    h = h_ref[...]
    for c in range(w_ref.shape[1] // chunk):
        sl = slice(c * chunk, (c + 1) * chunk)
        pre = _dot(h, w_ref[:, sl]) + b_ref[:, sl]
        g_ref[:, sl] = (1.0 / (1.0 + jnp.exp(-pre))).astype(BF16)


def _gate(h2d, wg, bg, tm):
    T, D = h2d.shape
    N = wg.shape[1]
    return pl.pallas_call(
        functools.partial(_gate_kernel, chunk=1024),
        grid=(T // tm,),
        in_specs=[pl.BlockSpec((tm, D), lambda i: (i, 0)), _resident(wg.shape), _resident(bg.shape)],
        out_specs=pl.BlockSpec((tm, N), lambda i: (i, 0)),
        out_shape=jax.ShapeDtypeStruct((T, N), BF16),
        compiler_params=_params(("parallel",)), name="gate",
    )(h2d, wg, bg)


def _memkv_kernel(mem_ref, nmem_ref, w_ref, o_ref):
    mn = _rms(mem_ref[0], nmem_ref[...]).astype(BF16)
    o_ref[0] = _dot(mn, w_ref[...]).astype(BF16)


def _memkv(mem, nmem, wxkv):
    B, M, D = mem.shape
    N = wxkv.shape[1]
    return pl.pallas_call(
        _memkv_kernel, grid=(B,),
        in_specs=[pl.BlockSpec((1, M, D), lambda b: (b, 0, 0)), _resident(nmem.shape), _resident(wxkv.shape)],
        out_specs=pl.BlockSpec((1, M, N), lambda b: (b, 0, 0)),
        out_shape=jax.ShapeDtypeStruct((B, M, N), BF16),
        compiler_params=_params(("parallel",)), name="memkv",
    )(mem, nmem, wxkv)


def _attn_kernel(q_ref, k_ref, v_ref, *rest, tk, nsub, ncast):
    w32, o_ref, w16 = rest[:ncast], rest[ncast], rest[ncast + 1:2 * ncast + 1]
    vx, s_a, s_b, m_s, acc_s = rest[2 * ncast + 1:]
    for src, dst in zip(w32, w16):
        dst[...] = src[...].astype(BF16)
    nh, dk = q_ref.shape[1], q_ref.shape[3]
    tq = q_ref.shape[2] // nsub
    S, dv = v_ref.shape[2], v_ref.shape[3]
    rows = nh * tq
    nchunks = S // tk
    ncol = tk // LANES
    assert dv == LANES, "alpha (lane-replicated) is applied to the accumulator without a broadcast"

    @pl.when(pl.program_id(2) == 0)
    def _():
        vx[:, 0:dv] = v_ref[0, 0]
        vx[:, dv:2 * dv] = jnp.ones((S, dv), BF16)

    def chunk(c):
        return pl.ds(c * tk, tk)

    for u in range(nsub):
        q = q_ref[0, :, u * tq:(u + 1) * tq, :].reshape(rows, dk)
        s_bufs = (s_a.at[u], s_b.at[u])
        m_u, acc_u = m_s.at[u], acc_s.at[u]

        def scores(c):
            return _dot_nt(q, k_ref[0, 0, chunk(c), :])

        def step(c):
            s_cur, s_nxt = s_bufs[c % 2], s_bufs[(c + 1) % 2]
            if c + 1 < nchunks:
                s_nxt[...] = scores(c + 1)
            cols = [s_cur[:, j * LANES:(j + 1) * LANES] for j in range(ncol)]
            mc = functools.reduce(jnp.maximum, cols)
            m_prev = m_u[...]
            m_new = jnp.maximum(m_prev, jnp.max(mc, axis=-1, keepdims=True))
            alpha = jnp.exp2(m_prev - m_new)
            p = jnp.concatenate([jnp.exp2((cj - m_new).astype(BF16)) for cj in cols], axis=-1)
            acc_u[...] = jnp.concatenate([alpha, alpha], axis=-1) * acc_u[...] + _dot(p, vx[chunk(c), :])
            m_u[...] = m_new

        m_u[...] = jnp.full(m_u.shape, NEG_BIG, F32)
        acc_u[...] = jnp.zeros(acc_u.shape, F32)
        s_bufs[0][...] = scores(0)
        for c in range(nchunks):
            step(c)
        o = acc_u[:, 0:dv] / acc_u[:, dv:2 * dv]
        for hh in range(nh):
            o_ref[0, u * tq:(u + 1) * tq, hh * dv:(hh + 1) * dv] = o[hh * tq:(hh + 1) * tq].astype(BF16)


def _cast_plan(w, nsteps):
    rows = w.shape[0]
    nb = nsteps
    while nb > 1 and (rows % nb or (rows // nb) % BF16_ROWS):
        nb //= 2
    return rows // nb, nsteps // nb


def _attention(q, k, v, tq, tk, nsub, casts=()):
    B, H, L, dk = q.shape
    G, S, dv = k.shape[1], k.shape[2], v.shape[3]
    nh = H // G
    tb = nsub * tq
    ni = L // tb
    plans = [_cast_plan(w, B * G * ni) for w in casts]

    def cast_spec(w, plan):
        rb, spb = plan
        return pl.BlockSpec((rb, w.shape[1]), lambda b, g, i: (((b * G + g) * ni + i) // spb, 0))

    cast_specs = [cast_spec(w, p) for w, p in zip(casts, plans)]
    return pl.pallas_call(
        functools.partial(_attn_kernel, tk=tk, nsub=nsub, ncast=len(casts)),
        grid=(B, G, ni),
        in_specs=[
            pl.BlockSpec((1, nh, tb, dk), lambda b, g, i: (b, g, i, 0)),
            pl.BlockSpec((1, 1, S, dk), lambda b, g, i: (b, g, 0, 0)),
            pl.BlockSpec((1, 1, S, dv), lambda b, g, i: (b, g, 0, 0)),
            *cast_specs,
        ],
        out_specs=(pl.BlockSpec((1, tb, nh * dv), lambda b, g, i: (b, i, g)), *cast_specs),
        out_shape=(jax.ShapeDtypeStruct((B, L, H * dv), BF16),
                   *[jax.ShapeDtypeStruct(w.shape, BF16) for w in casts]),
        scratch_shapes=[
            pltpu.VMEM((S, 2 * dv), BF16),
            pltpu.VMEM((nsub, nh * tq, tk), F32),
            pltpu.VMEM((nsub, nh * tq, tk), F32),
            pltpu.VMEM((nsub, nh * tq, LANES), F32),
            pltpu.VMEM((nsub, nh * tq, 2 * dv), F32),
        ],
        compiler_params=_params(("parallel", "parallel", "arbitrary")), name="attn",
    )(q, k, v, *casts)


def _mix_kernel(x_ref, g_ref, ya_ref, yb_ref, woa_ref, wob_ref, wout_ref, ncross_ref, wxq_ref, kvx_ref,
                wxo_ref, nffn_ref, x2_ref, hf_ref):
    D = x_ref.shape[2]
    a = _dot(ya_ref[0], woa_ref[...])
    b = _dot(yb_ref[0], wob_ref[...])
    m = g_ref[:, 0:D].astype(F32) * a + g_ref[:, D:2 * D].astype(F32) * b
    x1 = x_ref[0] + _dot(m.astype(BF16), wout_ref[...])

    hc = _rms(x1, ncross_ref[...]).astype(BF16)
    q = _dot(hc, wxq_ref[...]) * (X_HEAD_DIM ** -0.5)
    kw = X_HEADS * X_HEAD_DIM
    outs = []
    for hh in range(X_HEADS):
        sl = slice(X_HEAD_DIM * hh, X_HEAD_DIM * (hh + 1))
        s = _dot_nt(q[:, sl].astype(BF16), kvx_ref[0, :, sl])
        p = jnp.exp(s - jnp.max(s, axis=-1, keepdims=True))
        l = jnp.sum(p, axis=-1, keepdims=True)
        o = _dot(p.astype(BF16), kvx_ref[0, :, kw + X_HEAD_DIM * hh:kw + X_HEAD_DIM * (hh + 1)]) / l
        outs.append(o.astype(BF16))
    x2 = x1 + _dot(jnp.concatenate(outs, axis=-1), wxo_ref[...])
    x2_ref[0] = x2
    hf_ref[0] = _rms(x2, nffn_ref[...]).astype(BF16)


def _mix(x, g2d, ya, yb, woa, wob, wout, ncross, wxq, kvx, wxo, nffn, tm):
    B, L, D = x.shape
    nt = L // tm
    tok = lambda b, i: (b, i, 0)
    return pl.pallas_call(
        _mix_kernel, grid=(B, nt),
        in_specs=[
            pl.BlockSpec((1, tm, D), tok),
            pl.BlockSpec((tm, 2 * D), lambda b, i: (b * nt + i, 0)),
            pl.BlockSpec((1, tm, ya.shape[2]), tok),
            pl.BlockSpec((1, tm, yb.shape[2]), tok),
            _resident(woa.shape), _resident(wob.shape), _resident(wout.shape), _resident(ncross.shape),
            _resident(wxq.shape),
            pl.BlockSpec((1,) + kvx.shape[1:], lambda b, i: (b, 0, 0)),
            _resident(wxo.shape), _resident(nffn.shape),
        ],
        out_specs=(pl.BlockSpec((1, tm, D), tok), pl.BlockSpec((1, tm, D), tok)),
        out_shape=(jax.ShapeDtypeStruct((B, L, D), F32), jax.ShapeDtypeStruct((B, L, D), BF16)),
        compiler_params=_params(("parallel", "parallel")), name="mix",
    )(x, g2d, ya, yb, woa, wob, wout, ncross, wxq, kvx, wxo, nffn)


def _ffn_kernel(hm_ref, hp_ref, hn_ref, x2_ref, wa_ref, wb_ref, cwa_ref, cwb_ref, wd_ref, nfin_ref,
                o_ref, hbuf, ua, ub, *, tm):
    i, f = pl.program_id(1), pl.program_id(2)
    halo = BF16_ROWS

    @pl.when(f == 0)
    def _():
        prev, nxt = hp_ref[0], hn_ref[0]
        hbuf[0:halo] = jnp.where(i == 0, jnp.zeros_like(prev), prev)
        hbuf[halo:halo + tm] = hm_ref[0]
        hbuf[halo + tm:] = jnp.where(i == pl.num_programs(1) - 1, jnp.zeros_like(nxt), nxt)
        o_ref[...] = jnp.zeros_like(o_ref)

    hb = hbuf[...]

    def conv(u_ref, cw):
        return (u_ref[halo - 1:halo - 1 + tm] * cw[0:1] + u_ref[halo:halo + tm] * cw[1:2]
                + u_ref[halo + 1:halo + 1 + tm] * cw[2:3] + cw[3:4])

    ua[...] = _dot(hb, wa_ref[...])
    ub[...] = _dot(hb, wb_ref[...])
    a = conv(ua, cwa_ref[...])
    b = conv(ub, cwb_ref[...])
    act = a * (1.0 / (1.0 + jnp.exp(-a))) * b
    o_ref[0] += _dot(act.astype(BF16), wd_ref[...])

    @pl.when(f == pl.num_programs(2) - 1)
    def _():
        o_ref[0] = _rms(x2_ref[0] + o_ref[0], nfin_ref[...])


def _ffn(hf, x2, wup, cw, wdown, nfin, tm, tf):
    B, L, D = hf.shape
    dff = wdown.shape[0]
    nf = dff // tf
    hb = tm // BF16_ROWS
    last = L // BF16_ROWS - 1
    tok = lambda b, i, f: (b, i, 0)
    return pl.pallas_call(
        functools.partial(_ffn_kernel, tm=tm),
        grid=(B, L // tm, nf),
        in_specs=[
            pl.BlockSpec((1, tm, D), tok),
            pl.BlockSpec((1, BF16_ROWS, D), lambda b, i, f: (b, jnp.maximum(i * hb - 1, 0), 0)),
            pl.BlockSpec((1, BF16_ROWS, D), lambda b, i, f: (b, jnp.minimum((i + 1) * hb, last), 0)),
            pl.BlockSpec((1, tm, D), tok),
            pl.BlockSpec((D, tf), lambda b, i, f: (0, f)),
            pl.BlockSpec((D, tf), lambda b, i, f: (0, f + nf)),
            pl.BlockSpec((8, tf), lambda b, i, f: (0, f)),
            pl.BlockSpec((8, tf), lambda b, i, f: (0, f + nf)),
            pl.BlockSpec((tf, D), lambda b, i, f: (f, 0)),
            pl.BlockSpec(nfin.shape, lambda b, i, f: (0, 0)),
        ],
        out_specs=pl.BlockSpec((1, tm, D), tok),
        out_shape=jax.ShapeDtypeStruct((B, L, D), F32),
        scratch_shapes=[pltpu.VMEM((tm + 2 * BF16_ROWS, D), BF16),
                        pltpu.VMEM((tm + 2 * BF16_ROWS, tf), F32), pltpu.VMEM((tm + 2 * BF16_ROWS, tf), F32)],
        compiler_params=_params(("parallel", "parallel", "arbitrary")), name="ffn",
    )(hf, hf, hf, x2, wup, wup, cw, cw, wdown, nfin)


def _rope_tables(L):
    half = MLA_ROPE // 2
    inv = np.power(np.float32(ROPE_THETA), (-2.0 * np.arange(half, dtype=np.float32) / (2 * half)).astype(np.float32))
    t = np.arange(L)

    def cs(pos):
        ang = (pos.astype(np.float32)[:, None] * inv[None, :]).astype(np.float32)
        return np.cos(ang).astype(np.float32), np.sin(ang).astype(np.float32)

    z = np.zeros((L, half), np.float32)
    (ct, st), (cr, sr), (cc, sc) = cs(t), cs(t // GRID_W), cs(t % GRID_W)
    cat = lambda *p: np.concatenate(p, axis=-1)
    return jnp.asarray(np.stack([
        cat(ct, ct, z, z), cat(-st, z, z, z), cat(z, st, z, z),
        cat(cr, cr, cc, cc), cat(-sr, z, -sc, z), cat(z, sr, z, sc),
    ]))


def kernel(x, mem, norm_mix, w_in, mla_q_norm, w_uq, mla_kv_norm, w_ukv, gqa_q_norm, gqa_k_norm, w_o_mla,
           w_o_gqa, w_gate, b_gate, w_out, norm_cross, norm_mem, w_xq, w_xkv, w_xo, norm_ffn, w_up, conv_w,
           conv_b, w_down, norm_final):
    B, L, D = x.shape
    depth = w_in.shape[0]
    assert depth == 1, "the final norm is fused into the (single) layer's FFN kernel"
    tabs = _rope_tables(L)
    row = lambda v: v.reshape(1, -1).astype(F32)
    tm_prep = min(256, L)
    tm_gate = min(1024, L)
    tm_mix = min(256, L)
    tm_ffn = min(512, L)
    tq_a = min(512, L)
    tq_b = min(128, L)
    tk = min(512, L)
    nsub = 4 if L >= 4 * tq_a else 1

    for l in range(depth):
        wi = w_in[l]
        o_kr = MLA_Q_RANK + MLA_KV_RANK
        o_qb = o_kr + MLA_ROPE
        assert o_qb + (GQA_HEADS + 2 * GQA_KV_HEADS) * GQA_HEAD_DIM == wi.shape[1]
        wlat = wi[:, :o_kr].astype(BF16)
        wgqa = wi[:, o_qb:].astype(BF16)
        wkr = jnp.pad(wi[:, o_kr:o_qb], ((0, 0), (0, LANES - MLA_ROPE))).astype(BF16)
        wuq = jnp.pad(w_uq[l].reshape(MLA_Q_RANK, MLA_HEADS, MLA_NOPE + MLA_ROPE),
                      ((0, 0), (0, 0), (0, 2 * LANES - MLA_NOPE - MLA_ROPE))
                      ).reshape(MLA_Q_RANK, MLA_HEADS * 2 * LANES).astype(BF16)
        wkv = w_ukv[l].reshape(MLA_KV_RANK, MLA_HEADS, MLA_NOPE + MLA_V)
        wukv = jnp.concatenate([wkv[:, :, :MLA_NOPE].reshape(MLA_KV_RANK, -1),
                                wkv[:, :, MLA_NOPE:].reshape(MLA_KV_RANK, -1)], axis=1).astype(BF16)
        wx = w_xkv[l].reshape(D, X_HEADS, 2 * X_HEAD_DIM)
        wxkv = jnp.concatenate([wx[:, :, :X_HEAD_DIM].reshape(D, -1),
                                wx[:, :, X_HEAD_DIM:].reshape(D, -1)], axis=1).astype(BF16)
        cw = jnp.concatenate([conv_w[l], conv_b[l][None, :],
                              jnp.zeros((8 - CONV_W - 1, conv_w.shape[2]), F32)], axis=0)

        h, qm, km, vm, qg, kg, vg = _prep(
            x, row(norm_mix[l]), wlat, wgqa, wkr, row(mla_q_norm[l]), wuq, row(mla_kv_norm[l]), wukv,
            row(gqa_q_norm[l]), row(gqa_k_norm[l]), tabs, tm_prep)
        ya, wup, wg = _attention(qm, km, vm, tq_a, tk, nsub, casts=(w_up[l], w_gate[l]))
        yb, wdown, wout, woa, wob, wxq, wxo = _attention(
            qg, kg, vg, tq_b, tk, nsub,
            casts=(w_down[l], w_out[l], w_o_mla[l], w_o_gqa[l], w_xq[l], w_xo[l]))
        g = _gate(h.reshape(B * L, D), wg, row(b_gate[l]), tm_gate)
        kvx = _memkv(mem, row(norm_mem[l]), wxkv)
        x2, hf = _mix(x, g, ya, yb, woa, wob, wout, row(norm_cross[l]), wxq, kvx, wxo, row(norm_ffn[l]), tm_mix)
        x = _ffn(hf, x2, wup, cw, wdown, row(norm_final), tm_ffn, min(512, w_down.shape[1]))
    return x
```

```python
import functools

import jax
import jax.numpy as jnp
import numpy as np
from jax import lax
from jax.experimental import pallas as pl
from jax.experimental.pallas import tpu as pltpu

F32 = jnp.float32
BF16 = jnp.bfloat16

EPS = 1e-6
ROPE_THETA = 10000.0
GRID_W = 64
MLA_HEADS = 8
MLA_Q_RANK = 512
MLA_KV_RANK = 512
MLA_NOPE = 128
MLA_ROPE = 64
MLA_V = 128
GQA_HEADS = 8
GQA_KV_HEADS = 2
GQA_HEAD_DIM = 128
X_HEADS = 4
X_HEAD_DIM = 128
CONV_W = 3

LANES = 128
BF16_ROWS = 16
VMEM_LIMIT = 56 * 1024 * 1024
NEG_BIG = -1e30
LOG2E = float(np.log2(np.e))


def _params(semantics):
    return pltpu.CompilerParams(dimension_semantics=semantics, vmem_limit_bytes=VMEM_LIMIT)


def _resident(shape):
    zeros = (0,) * len(shape)
    return pl.BlockSpec(shape, lambda *_: zeros, pipeline_mode=pl.Buffered(1))


def _rms(x, gain):
    ms = jnp.mean(x * x, axis=-1, keepdims=True)
    return x * lax.rsqrt(ms + EPS) * gain


def _dot(a, b):
    return jnp.dot(a, b, preferred_element_type=F32)


def _dot_nt(a, b):
    return lax.dot_general(a, b, (((1,), (1,)), ((), ())), preferred_element_type=F32)


def _rope(t, c, sa, sb):
    return t * c + pltpu.roll(t, LANES - 32, 1) * sa + pltpu.roll(t, 32, 1) * sb


def _prep_kernel(x_ref, nmix_ref, wlat_ref, wgqa_ref, wkr_ref, qn_ref, wuq_ref, kvn_ref, wukv_ref, gq_ref,
                 gk_ref, tab_ref, h_ref, qm_ref, km_ref, vm_ref, qg_ref, kg_ref, vg_ref):
    x = x_ref[0]
    h = _rms(x, nmix_ref[...]).astype(BF16)
    h_ref[0] = h
    zl = _dot(h, wlat_ref[...])
    zg = _dot(h, wgqa_ref[...])
    zr = _dot(h, wkr_ref[...])
    c1, sa1, sb1 = tab_ref[0], tab_ref[1], tab_ref[2]
    c2, sa2, sb2 = tab_ref[3], tab_ref[4], tab_ref[5]

    scale_a = (MLA_NOPE + MLA_ROPE) ** -0.5 * LOG2E
    cq = _rms(zl[:, 0:MLA_Q_RANK], qn_ref[...]).astype(BF16)
    q = _dot(cq, wuq_ref[...])
    for hh in range(MLA_HEADS):
        base = 2 * LANES * hh
        qm_ref[0, hh, :, 0:LANES] = (q[:, base:base + LANES] * scale_a).astype(BF16)
        r = _rope(q[:, base + LANES:base + 2 * LANES], c1, sa1, sb1)
        qm_ref[0, hh, :, LANES:2 * LANES] = (r * scale_a).astype(BF16)

    ckv = _rms(zl[:, MLA_Q_RANK:MLA_Q_RANK + MLA_KV_RANK], kvn_ref[...]).astype(BF16)
    kv = _dot(ckv, wukv_ref[...])
    kr = _rope(zr, c1, sa1, sb1).astype(BF16)
    for hh in range(MLA_HEADS):
        km_ref[0, hh, :, 0:LANES] = kv[:, LANES * hh:LANES * (hh + 1)].astype(BF16)
        km_ref[0, hh, :, LANES:2 * LANES] = kr
        vb = MLA_HEADS * MLA_NOPE + MLA_V * hh
        vm_ref[0, hh] = kv[:, vb:vb + MLA_V].astype(BF16)

    scale_b = GQA_HEAD_DIM ** -0.5 * LOG2E
    dh = GQA_HEAD_DIM
    o_k = GQA_HEADS * dh
    o_v = o_k + GQA_KV_HEADS * dh
    for hh in range(GQA_HEADS):
        t = _rms(zg[:, dh * hh:dh * (hh + 1)], gq_ref[...])
        qg_ref[0, hh] = (_rope(t, c2, sa2, sb2) * scale_b).astype(BF16)
    for hh in range(GQA_KV_HEADS):
        t = _rms(zg[:, o_k + dh * hh:o_k + dh * (hh + 1)], gk_ref[...])
        kg_ref[0, hh] = _rope(t, c2, sa2, sb2).astype(BF16)
        vg_ref[0, hh] = zg[:, o_v + dh * hh:o_v + dh * (hh + 1)].astype(BF16)


def _prep(x, nmix, wlat, wgqa, wkr, qn, wuq, kvn, wukv, gq, gk, tabs, tm):
    B, L, D = x.shape
    grid = (B, L // tm)
    tok = lambda b, i: (b, i, 0)
    head = lambda b, i: (b, 0, i, 0)
    out_shape = (
        jax.ShapeDtypeStruct((B, L, D), BF16),
        jax.ShapeDtypeStruct((B, MLA_HEADS, L, 2 * LANES), BF16),
        jax.ShapeDtypeStruct((B, MLA_HEADS, L, 2 * LANES), BF16),
        jax.ShapeDtypeStruct((B, MLA_HEADS, L, MLA_V), BF16),
        jax.ShapeDtypeStruct((B, GQA_HEADS, L, GQA_HEAD_DIM), BF16),
        jax.ShapeDtypeStruct((B, GQA_KV_HEADS, L, GQA_HEAD_DIM), BF16),
        jax.ShapeDtypeStruct((B, GQA_KV_HEADS, L, GQA_HEAD_DIM), BF16),
    )
    out_specs = (
        pl.BlockSpec((1, tm, D), tok),
        pl.BlockSpec((1, MLA_HEADS, tm, 2 * LANES), head),
        pl.BlockSpec((1, MLA_HEADS, tm, 2 * LANES), head),
        pl.BlockSpec((1, MLA_HEADS, tm, MLA_V), head),
        pl.BlockSpec((1, GQA_HEADS, tm, GQA_HEAD_DIM), head),
        pl.BlockSpec((1, GQA_KV_HEADS, tm, GQA_HEAD_DIM), head),
        pl.BlockSpec((1, GQA_KV_HEADS, tm, GQA_HEAD_DIM), head),
    )
    in_specs = [
        pl.BlockSpec((1, tm, D), tok),
        _resident(nmix.shape), _resident(wlat.shape), _resident(wgqa.shape), _resident(wkr.shape),
        _resident(qn.shape), _resident(wuq.shape),
        _resident(kvn.shape), _resident(wukv.shape), _resident(gq.shape), _resident(gk.shape),
        pl.BlockSpec((6, tm, LANES), lambda b, i: (0, i, 0)),
    ]
    return pl.pallas_call(
        _prep_kernel, grid=grid, in_specs=in_specs, out_specs=out_specs, out_shape=out_shape,
        compiler_params=_params(("parallel", "parallel")), name="prep",
    )(x, nmix, wlat, wgqa, wkr, qn, wuq, kvn, wukv, gq, gk, tabs)


def _gate_kernel(h_ref, w_ref, b_ref, g_ref, *, chunk):
    h = h_ref[...]
    for c in range(w_ref.shape[1] // chunk):
        sl = slice(c * chunk, (c + 1) * chunk)
        pre = _dot(h, w_ref[:, sl]) + b_ref[:, sl]
        g_ref[:, sl] = (1.0 / (1.0 + jnp.exp(-pre))).astype(BF16)


def _gate(h2d, wg, bg, tm):
    T, D = h2d.shape
    N = wg.shape[1]
    return pl.pallas_call(
        functools.partial(_gate_kernel, chunk=1024),
        grid=(T // tm,),
        in_specs=[pl.BlockSpec((tm, D), lambda i: (i, 0)), _resident(wg.shape), _resident(bg.shape)],
        out_specs=pl.BlockSpec((tm, N), lambda i: (i, 0)),
        out_shape=jax.ShapeDtypeStruct((T, N), BF16),
        compiler_params=_params(("parallel",)), name="gate",
    )(h2d, wg, bg)


def _memkv_kernel(mem_ref, nmem_ref, w_ref, o_ref):
    mn = _rms(mem_ref[0], nmem_ref[...]).astype(BF16)
    o_ref[0] = _dot(mn, w_ref[...]).astype(BF16)


def _memkv(mem, nmem, wxkv):
    B, M, D = mem.shape
    N = wxkv.shape[1]
    return pl.pallas_call(
        _memkv_kernel, grid=(B,),
        in_specs=[pl.BlockSpec((1, M, D), lambda b: (b, 0, 0)), _resident(nmem.shape), _resident(wxkv.shape)],
        out_specs=pl.BlockSpec((1, M, N), lambda b: (b, 0, 0)),
        out_shape=jax.ShapeDtypeStruct((B, M, N), BF16),
        compiler_params=_params(("parallel",)), name="memkv",
    )(mem, nmem, wxkv)


def _attn_kernel(q_ref, k_ref, v_ref, *rest, tk, nsub, ncast):
    w32, o_ref, w16 = rest[:ncast], rest[ncast], rest[ncast + 1:2 * ncast + 1]
    vx, s_a, s_b, m_s, acc_s = rest[2 * ncast + 1:]
    for src, dst in zip(w32, w16):
        dst[...] = src[...].astype(BF16)
    nh, dk = q_ref.shape[1], q_ref.shape[3]
    tq = q_ref.shape[2] // nsub
    S, dv = v_ref.shape[2], v_ref.shape[3]
    rows = nh * tq
    nchunks = S // tk
    ncol = tk // LANES
    assert dv == LANES, "alpha (lane-replicated) is applied to the accumulator without a broadcast"

    @pl.when(pl.program_id(2) == 0)
    def _():
        vx[:, 0:dv] = v_ref[0, 0]
        vx[:, dv:2 * dv] = jnp.ones((S, dv), BF16)

    def chunk(c):
        return pl.ds(c * tk, tk)

    for u in range(nsub):
        q = q_ref[0, :, u * tq:(u + 1) * tq, :].reshape(rows, dk)
        s_bufs = (s_a.at[u], s_b.at[u])
        m_u, acc_u = m_s.at[u], acc_s.at[u]

        def scores(c):
            return _dot_nt(q, k_ref[0, 0, chunk(c), :])

        def step(c):
            s_cur, s_nxt = s_bufs[c % 2], s_bufs[(c + 1) % 2]
            if c + 1 < nchunks:
                s_nxt[...] = scores(c + 1)
            cols = [s_cur[:, j * LANES:(j + 1) * LANES] for j in range(ncol)]
            mc = functools.reduce(jnp.maximum, cols)
            m_prev = m_u[...]
            m_new = jnp.maximum(m_prev, jnp.max(mc, axis=-1, keepdims=True))
            alpha = jnp.exp2(m_prev - m_new)
            p = jnp.concatenate([jnp.exp2((cj - m_new).astype(BF16)) for cj in cols], axis=-1)
            acc_u[...] = jnp.concatenate([alpha, alpha], axis=-1) * acc_u[...] + _dot(p, vx[chunk(c), :])
            m_u[...] = m_new

        m_u[...] = jnp.full(m_u.shape, NEG_BIG, F32)
        acc_u[...] = jnp.zeros(acc_u.shape, F32)
        s_bufs[0][...] = scores(0)
        for c in range(nchunks):
            step(c)
        o = acc_u[:, 0:dv] / acc_u[:, dv:2 * dv]
        for hh in range(nh):
            o_ref[0, u * tq:(u + 1) * tq, hh * dv:(hh + 1) * dv] = o[hh * tq:(hh + 1) * tq].astype(BF16)


def _cast_plan(w, nsteps):
    rows = w.shape[0]
    nb = nsteps
    while nb > 1 and (rows % nb or (rows // nb) % BF16_ROWS):
        nb //= 2
    return rows // nb, nsteps // nb


def _attention(q, k, v, tq, tk, nsub, casts=()):
    B, H, L, dk = q.shape
    G, S, dv = k.shape[1], k.shape[2], v.shape[3]
    nh = H // G
    tb = nsub * tq
    ni = L // tb
    plans = [_cast_plan(w, B * G * ni) for w in casts]

    def cast_spec(w, plan):
        rb, spb = plan
        return pl.BlockSpec((rb, w.shape[1]), lambda b, g, i: (((b * G + g) * ni + i) // spb, 0))

    cast_specs = [cast_spec(w, p) for w, p in zip(casts, plans)]
    return pl.pallas_call(
        functools.partial(_attn_kernel, tk=tk, nsub=nsub, ncast=len(casts)),
        grid=(B, G, ni),
        in_specs=[
            pl.BlockSpec((1, nh, tb, dk), lambda b, g, i: (b, g, i, 0)),
            pl.BlockSpec((1, 1, S, dk), lambda b, g, i: (b, g, 0, 0)),
            pl.BlockSpec((1, 1, S, dv), lambda b, g, i: (b, g, 0, 0)),
            *cast_specs,
        ],
        out_specs=(pl.BlockSpec((1, tb, nh * dv), lambda b, g, i: (b, i, g)), *cast_specs),
        out_shape=(jax.ShapeDtypeStruct((B, L, H * dv), BF16),
                   *[jax.ShapeDtypeStruct(w.shape, BF16) for w in casts]),
        scratch_shapes=[
            pltpu.VMEM((S, 2 * dv), BF16),
            pltpu.VMEM((nsub, nh * tq, tk), F32),
            pltpu.VMEM((nsub, nh * tq, tk), F32),
            pltpu.VMEM((nsub, nh * tq, LANES), F32),
            pltpu.VMEM((nsub, nh * tq, 2 * dv), F32),
        ],
        compiler_params=_params(("parallel", "parallel", "arbitrary")), name="attn",
    )(q, k, v, *casts)


def _mix_kernel(x_ref, g_ref, ya_ref, yb_ref, woa_ref, wob_ref, wout_ref, ncross_ref, wxq_ref, kvx_ref,
                wxo_ref, nffn_ref, x2_ref, hf_ref):
    D = x_ref.shape[2]
    a = _dot(ya_ref[0], woa_ref[...])
    b = _dot(yb_ref[0], wob_ref[...])
    m = g_ref[:, 0:D].astype(F32) * a + g_ref[:, D:2 * D].astype(F32) * b
    x1 = x_ref[0] + _dot(m.astype(BF16), wout_ref[...])

    hc = _rms(x1, ncross_ref[...]).astype(BF16)
    q = _dot(hc, wxq_ref[...]) * (X_HEAD_DIM ** -0.5)
    kw = X_HEADS * X_HEAD_DIM
    outs = []
    for hh in range(X_HEADS):
        sl = slice(X_HEAD_DIM * hh, X_HEAD_DIM * (hh + 1))
        s = _dot_nt(q[:, sl].astype(BF16), kvx_ref[0, :, sl])
        p = jnp.exp(s - jnp.max(s, axis=-1, keepdims=True))
        l = jnp.sum(p, axis=-1, keepdims=True)
        o = _dot(p.astype(BF16), kvx_ref[0, :, kw + X_HEAD_DIM * hh:kw + X_HEAD_DIM * (hh + 1)]) / l
        outs.append(o.astype(BF16))
    x2 = x1 + _dot(jnp.concatenate(outs, axis=-1), wxo_ref[...])
    x2_ref[0] = x2
    hf_ref[0] = _rms(x2, nffn_ref[...]).astype(BF16)


def _mix(x, g2d, ya, yb, woa, wob, wout, ncross, wxq, kvx, wxo, nffn, tm):
    B, L, D = x.shape
    nt = L // tm
    tok = lambda b, i: (b, i, 0)
    return pl.pallas_call(
        _mix_kernel, grid=(B, nt),
        in_specs=[
            pl.BlockSpec((1, tm, D), tok),
            pl.BlockSpec((tm, 2 * D), lambda b, i: (b * nt + i, 0)),
            pl.BlockSpec((1, tm, ya.shape[2]), tok),
            pl.BlockSpec((1, tm, yb.shape[2]), tok),
            _resident(woa.shape), _resident(wob.shape), _resident(wout.shape), _resident(ncross.shape),
            _resident(wxq.shape),
            pl.BlockSpec((1,) + kvx.shape[1:], lambda b, i: (b, 0, 0)),
            _resident(wxo.shape), _resident(nffn.shape),
        ],
        out_specs=(pl.BlockSpec((1, tm, D), tok), pl.BlockSpec((1, tm, D), tok)),
        out_shape=(jax.ShapeDtypeStruct((B, L, D), F32), jax.ShapeDtypeStruct((B, L, D), BF16)),
        compiler_params=_params(("parallel", "parallel")), name="mix",
    )(x, g2d, ya, yb, woa, wob, wout, ncross, wxq, kvx, wxo, nffn)


def _ffn_kernel(hm_ref, hp_ref, hn_ref, x2_ref, wa_ref, wb_ref, cwa_ref, cwb_ref, wd_ref, nfin_ref,
                o_ref, hbuf, ua, ub, *, tm):
    i, f = pl.program_id(1), pl.program_id(2)
    halo = BF16_ROWS

    @pl.when(f == 0)
    def _():
        prev, nxt = hp_ref[0], hn_ref[0]
        hbuf[0:halo] = jnp.where(i == 0, jnp.zeros_like(prev), prev)
        hbuf[halo:halo + tm] = hm_ref[0]
        hbuf[halo + tm:] = jnp.where(i == pl.num_programs(1) - 1, jnp.zeros_like(nxt), nxt)
        o_ref[...] = jnp.zeros_like(o_ref)

    hb = hbuf[...]

    def conv(u_ref, cw):
        return (u_ref[halo:halo + tm] * cw[1:2] + cw[3:4] + u_ref[halo - 1:halo - 1 + tm] * cw[0:1]
                + u_ref[halo + 1:halo + 1 + tm] * cw[2:3])

    ua[...] = _dot(hb, wa_ref[...])
    ub[...] = _dot(hb, wb_ref[...])
    a = conv(ua, cwa_ref[...])
    b = conv(ub, cwb_ref[...])
    act = a * (1.0 / (1.0 + jnp.exp(-a))) * b
    o_ref[0] += _dot(act.astype(BF16), wd_ref[...])

    @pl.when(f == pl.num_programs(2) - 1)
    def _():
        o_ref[0] = _rms(x2_ref[0] + o_ref[0], nfin_ref[...])


def _ffn(hf, x2, wup, cw, wdown, nfin, tm, tf):
    B, L, D = hf.shape
    dff = wdown.shape[0]
    nf = dff // tf
    hb = tm // BF16_ROWS
    last = L // BF16_ROWS - 1
    tok = lambda b, i, f: (b, i, 0)
    return pl.pallas_call(
        functools.partial(_ffn_kernel, tm=tm),
        grid=(B, L // tm, nf),
        in_specs=[
            pl.BlockSpec((1, tm, D), tok),
            pl.BlockSpec((1, BF16_ROWS, D), lambda b, i, f: (b, jnp.maximum(i * hb - 1, 0), 0)),
            pl.BlockSpec((1, BF16_ROWS, D), lambda b, i, f: (b, jnp.minimum((i + 1) * hb, last), 0)),
            pl.BlockSpec((1, tm, D), tok),
            pl.BlockSpec((D, tf), lambda b, i, f: (0, f)),
            pl.BlockSpec((D, tf), lambda b, i, f: (0, f + nf)),
            pl.BlockSpec((8, tf), lambda b, i, f: (0, f)),
            pl.BlockSpec((8, tf), lambda b, i, f: (0, f + nf)),
            pl.BlockSpec((tf, D), lambda b, i, f: (f, 0)),
            pl.BlockSpec(nfin.shape, lambda b, i, f: (0, 0)),
        ],
        out_specs=pl.BlockSpec((1, tm, D), tok),
        out_shape=jax.ShapeDtypeStruct((B, L, D), F32),
        scratch_shapes=[pltpu.VMEM((tm + 2 * BF16_ROWS, D), BF16),
                        pltpu.VMEM((tm + 2 * BF16_ROWS, tf), F32), pltpu.VMEM((tm + 2 * BF16_ROWS, tf), F32)],
        compiler_params=_params(("parallel", "parallel", "arbitrary")), name="ffn",
    )(hf, hf, hf, x2, wup, wup, cw, cw, wdown, nfin)


def _rope_tables(L):
    half = MLA_ROPE // 2
    inv = np.power(np.float32(ROPE_THETA), (-2.0 * np.arange(half, dtype=np.float32) / (2 * half)).astype(np.float32))
    t = np.arange(L)

    def cs(pos):
        ang = (pos.astype(np.float32)[:, None] * inv[None, :]).astype(np.float32)
        return np.cos(ang).astype(np.float32), np.sin(ang).astype(np.float32)

    z = np.zeros((L, half), np.float32)
    (ct, st), (cr, sr), (cc, sc) = cs(t), cs(t // GRID_W), cs(t % GRID_W)
    cat = lambda *p: np.concatenate(p, axis=-1)
    return jnp.asarray(np.stack([
        cat(ct, ct, z, z), cat(-st, z, z, z), cat(z, st, z, z),
        cat(cr, cr, cc, cc), cat(-sr, z, -sc, z), cat(z, sr, z, sc),
    ]))


def kernel(x, mem, norm_mix, w_in, mla_q_norm, w_uq, mla_kv_norm, w_ukv, gqa_q_norm, gqa_k_norm, w_o_mla,
           w_o_gqa, w_gate, b_gate, w_out, norm_cross, norm_mem, w_xq, w_xkv, w_xo, norm_ffn, w_up, conv_w,
           conv_b, w_down, norm_final):
    B, L, D = x.shape
    depth = w_in.shape[0]
    assert depth == 1, "the final norm is fused into the (single) layer's FFN kernel"
    tabs = _rope_tables(L)
    row = lambda v: v.reshape(1, -1).astype(F32)
    tm_prep = min(256, L)
    tm_gate = min(1024, L)
    tm_mix = min(256, L)
    tm_ffn = min(512, L)
    tq_a = min(512, L)
    tq_b = min(128, L)
    tk = min(512, L)
    nsub = 8 if L >= 8 * tq_a else 1

    for l in range(depth):
        wi = w_in[l]
        o_kr = MLA_Q_RANK + MLA_KV_RANK
        o_qb = o_kr + MLA_ROPE
        assert o_qb + (GQA_HEADS + 2 * GQA_KV_HEADS) * GQA_HEAD_DIM == wi.shape[1]
        wlat = wi[:, :o_kr].astype(BF16)
        wgqa = wi[:, o_qb:].astype(BF16)
        wkr = jnp.pad(wi[:, o_kr:o_qb], ((0, 0), (0, LANES - MLA_ROPE))).astype(BF16)
        wuq = jnp.pad(w_uq[l].reshape(MLA_Q_RANK, MLA_HEADS, MLA_NOPE + MLA_ROPE),
                      ((0, 0), (0, 0), (0, 2 * LANES - MLA_NOPE - MLA_ROPE))
                      ).reshape(MLA_Q_RANK, MLA_HEADS * 2 * LANES).astype(BF16)
        wkv = w_ukv[l].reshape(MLA_KV_RANK, MLA_HEADS, MLA_NOPE + MLA_V)
        wukv = jnp.concatenate([wkv[:, :, :MLA_NOPE].reshape(MLA_KV_RANK, -1),
                                wkv[:, :, MLA_NOPE:].reshape(MLA_KV_RANK, -1)], axis=1).astype(BF16)
        wx = w_xkv[l].reshape(D, X_HEADS, 2 * X_HEAD_DIM)
        wxkv = jnp.concatenate([wx[:, :, :X_HEAD_DIM].reshape(D, -1),
                                wx[:, :, X_HEAD_DIM:].reshape(D, -1)], axis=1).astype(BF16)
        cw = jnp.concatenate([conv_w[l], conv_b[l][None, :],
                              jnp.zeros((8 - CONV_W - 1, conv_w.shape[2]), F32)], axis=0)

        h, qm, km, vm, qg, kg, vg = _prep(
            x, row(norm_mix[l]), wlat, wgqa, wkr, row(mla_q_norm[l]), wuq, row(mla_kv_norm[l]), wukv,
            row(gqa_q_norm[l]), row(gqa_k_norm[l]), tabs, tm_prep)
        ya, wup, wg = _attention(qm, km, vm, tq_a, tk, nsub, casts=(w_up[l], w_gate[l]))
        yb, wdown, wout, woa, wob, wxq, wxo = _attention(
            qg, kg, vg, tq_b, tk, nsub,
            casts=(w_down[l], w_out[l], w_o_mla[l], w_o_gqa[l], w_xq[l], w_xo[l]))
        g = _gate(h.reshape(B * L, D), wg, row(b_gate[l]), tm_gate)
        kvx = _memkv(mem, row(norm_mem[l]), wxkv)
        x2, hf = _mix(x, g, ya, yb, woa, wob, wout, row(norm_cross[l]), wxq, kvx, wxo, row(norm_ffn[l]), tm_mix)
        x = _ffn(hf, x2, wup, cw, wdown, row(norm_final), tm_ffn, min(512, w_down.shape[1]))
    return x
```

```python
import functools

import jax
import jax.numpy as jnp
import numpy as np
from jax import lax
from jax.experimental import pallas as pl
from jax.experimental.pallas import tpu as pltpu

F32 = jnp.float32
BF16 = jnp.bfloat16

EPS = 1e-6
ROPE_THETA = 10000.0
GRID_W = 64
MLA_HEADS = 8
MLA_Q_RANK = 512
MLA_KV_RANK = 512
MLA_NOPE = 128
MLA_ROPE = 64
MLA_V = 128
GQA_HEADS = 8
GQA_KV_HEADS = 2
GQA_HEAD_DIM = 128
X_HEADS = 4
X_HEAD_DIM = 128
CONV_W = 3

LANES = 128
BF16_ROWS = 16
VMEM_LIMIT = 56 * 1024 * 1024
NEG_BIG = -1e30
LOG2E = float(np.log2(np.e))


def _params(semantics):
    return pltpu.CompilerParams(dimension_semantics=semantics, vmem_limit_bytes=VMEM_LIMIT)


def _resident(shape):
    zeros = (0,) * len(shape)
    return pl.BlockSpec(shape, lambda *_: zeros, pipeline_mode=pl.Buffered(1))


def _rms(x, gain):
    ms = jnp.mean(x * x, axis=-1, keepdims=True)
    return x * lax.rsqrt(ms + EPS) * gain


def _dot(a, b):
    return jnp.dot(a, b, preferred_element_type=F32)


def _dot_nt(a, b):
    return lax.dot_general(a, b, (((1,), (1,)), ((), ())), preferred_element_type=F32)


def _rope(t, c, sa, sb):
    return t * c + pltpu.roll(t, LANES - 32, 1) * sa + pltpu.roll(t, 32, 1) * sb


def _prep_kernel(x_ref, nmix_ref, wlat_ref, wgqa_ref, wkr_ref, qn_ref, wuq_ref, kvn_ref, wukv_ref, gq_ref,
                 gk_ref, tab_ref, h_ref, qm_ref, km_ref, vm_ref, qg_ref, kg_ref, vg_ref):
    x = x_ref[0]
    h = _rms(x, nmix_ref[...]).astype(BF16)
    h_ref[0] = h
    zl = _dot(h, wlat_ref[...])
    zg = _dot(h, wgqa_ref[...])
    zr = _dot(h, wkr_ref[...])
    c1, sa1, sb1 = tab_ref[0], tab_ref[1], tab_ref[2]
    c2, sa2, sb2 = tab_ref[3], tab_ref[4], tab_ref[5]

    scale_a = (MLA_NOPE + MLA_ROPE) ** -0.5 * LOG2E
    cq = _rms(zl[:, 0:MLA_Q_RANK], qn_ref[...]).astype(BF16)
    q = _dot(cq, wuq_ref[...])
    for hh in range(MLA_HEADS):
        base = 2 * LANES * hh
        qm_ref[0, hh, :, 0:LANES] = (q[:, base:base + LANES] * scale_a).astype(BF16)
        r = _rope(q[:, base + LANES:base + 2 * LANES], c1, sa1, sb1)
        qm_ref[0, hh, :, LANES:2 * LANES] = (r * scale_a).astype(BF16)

    ckv = _rms(zl[:, MLA_Q_RANK:MLA_Q_RANK + MLA_KV_RANK], kvn_ref[...]).astype(BF16)
    kv = _dot(ckv, wukv_ref[...])
    kr = _rope(zr, c1, sa1, sb1).astype(BF16)
    for hh in range(MLA_HEADS):
        km_ref[0, hh, :, 0:LANES] = kv[:, LANES * hh:LANES * (hh + 1)].astype(BF16)
        km_ref[0, hh, :, LANES:2 * LANES] = kr
        vb = MLA_HEADS * MLA_NOPE + MLA_V * hh
        vm_ref[0, hh] = kv[:, vb:vb + MLA_V].astype(BF16)

    scale_b = GQA_HEAD_DIM ** -0.5 * LOG2E
    dh = GQA_HEAD_DIM
    o_k = GQA_HEADS * dh
    o_v = o_k + GQA_KV_HEADS * dh
    for hh in range(GQA_HEADS):
        t = _rms(zg[:, dh * hh:dh * (hh + 1)], gq_ref[...])
        qg_ref[0, hh] = (_rope(t, c2, sa2, sb2) * scale_b).astype(BF16)
    for hh in range(GQA_KV_HEADS):
        t = _rms(zg[:, o_k + dh * hh:o_k + dh * (hh + 1)], gk_ref[...])
        kg_ref[0, hh] = _rope(t, c2, sa2, sb2).astype(BF16)
        vg_ref[0, hh] = zg[:, o_v + dh * hh:o_v + dh * (hh + 1)].astype(BF16)


def _prep(x, nmix, wlat, wgqa, wkr, qn, wuq, kvn, wukv, gq, gk, tabs, tm):
    B, L, D = x.shape
    grid = (B, L // tm)
    tok = lambda b, i: (b, i, 0)
    head = lambda b, i: (b, 0, i, 0)
    out_shape = (
        jax.ShapeDtypeStruct((B, L, D), BF16),
        jax.ShapeDtypeStruct((B, MLA_HEADS, L, 2 * LANES), BF16),
        jax.ShapeDtypeStruct((B, MLA_HEADS, L, 2 * LANES), BF16),
        jax.ShapeDtypeStruct((B, MLA_HEADS, L, MLA_V), BF16),
        jax.ShapeDtypeStruct((B, GQA_HEADS, L, GQA_HEAD_DIM), BF16),
        jax.ShapeDtypeStruct((B, GQA_KV_HEADS, L, GQA_HEAD_DIM), BF16),
        jax.ShapeDtypeStruct((B, GQA_KV_HEADS, L, GQA_HEAD_DIM), BF16),
    )
    out_specs = (
        pl.BlockSpec((1, tm, D), tok),
        pl.BlockSpec((1, MLA_HEADS, tm, 2 * LANES), head),
        pl.BlockSpec((1, MLA_HEADS, tm, 2 * LANES), head),
        pl.BlockSpec((1, MLA_HEADS, tm, MLA_V), head),
        pl.BlockSpec((1, GQA_HEADS, tm, GQA_HEAD_DIM), head),
        pl.BlockSpec((1, GQA_KV_HEADS, tm, GQA_HEAD_DIM), head),
        pl.BlockSpec((1, GQA_KV_HEADS, tm, GQA_HEAD_DIM), head),
    )
    in_specs = [
        pl.BlockSpec((1, tm, D), tok),
        _resident(nmix.shape), _resident(wlat.shape), _resident(wgqa.shape), _resident(wkr.shape),
        _resident(qn.shape), _resident(wuq.shape),
        _resident(kvn.shape), _resident(wukv.shape), _resident(gq.shape), _resident(gk.shape),
        pl.BlockSpec((6, tm, LANES), lambda b, i: (0, i, 0)),
    ]
    return pl.pallas_call(
        _prep_kernel, grid=grid, in_specs=in_specs, out_specs=out_specs, out_shape=out_shape,
        compiler_params=_params(("parallel", "parallel")), name="prep",
    )(x, nmix, wlat, wgqa, wkr, qn, wuq, kvn, wukv, gq, gk, tabs)


def _gate_kernel(h_ref, w_ref, b_ref, g_ref, *, chunk):
    h = h_ref[...]
    for c in range(w_ref.shape[1] // chunk):
        sl = slice(c * chunk, (c + 1) * chunk)
        pre = _dot(h, w_ref[:, sl]) + b_ref[:, sl]
        g_ref[:, sl] = (1.0 / (1.0 + jnp.exp(-pre))).astype(BF16)


def _gate(h2d, wg, bg, tm):
    T, D = h2d.shape
    N = wg.shape[1]
    return pl.pallas_call(
        functools.partial(_gate_kernel, chunk=1024),
        grid=(T // tm,),
        in_specs=[pl.BlockSpec((tm, D), lambda i: (i, 0)), _resident(wg.shape), _resident(bg.shape)],
        out_specs=pl.BlockSpec((tm, N), lambda i: (i, 0)),
        out_shape=jax.ShapeDtypeStruct((T, N), BF16),
        compiler_params=_params(("parallel",)), name="gate",
    )(h2d, wg, bg)


def _memkv_kernel(mem_ref, nmem_ref, w_ref, o_ref):
    mn = _rms(mem_ref[0], nmem_ref[...]).astype(BF16)
    o_ref[0] = _dot(mn, w_ref[...]).astype(BF16)


def _memkv(mem, nmem, wxkv):
    B, M, D = mem.shape
    N = wxkv.shape[1]
    return pl.pallas_call(
        _memkv_kernel, grid=(B,),
        in_specs=[pl.BlockSpec((1, M, D), lambda b: (b, 0, 0)), _resident(nmem.shape), _resident(wxkv.shape)],
        out_specs=pl.BlockSpec((1, M, N), lambda b: (b, 0, 0)),
        out_shape=jax.ShapeDtypeStruct((B, M, N), BF16),
        compiler_params=_params(("parallel",)), name="memkv",
    )(mem, nmem, wxkv)


def _attn_kernel(q_ref, k_ref, v_ref, *rest, tk, nsub, ncast):
    w32, o_ref, w16 = rest[:ncast], rest[ncast], rest[ncast + 1:2 * ncast + 1]
    vx, s_a, s_b, m_s, acc_s = rest[2 * ncast + 1:]
    for src, dst in zip(w32, w16):
        dst[...] = src[...].astype(BF16)
    nh, dk = q_ref.shape[1], q_ref.shape[3]
    tq = q_ref.shape[2] // nsub
    S, dv = v_ref.shape[2], v_ref.shape[3]
    rows = nh * tq
    nchunks = S // tk
    ncol = tk // LANES
    assert dv == LANES, "alpha (lane-replicated) is applied to the accumulator without a broadcast"

    @pl.when(pl.program_id(2) == 0)
    def _():
        vx[:, 0:dv] = v_ref[0, 0]
        vx[:, dv:2 * dv] = jnp.ones((S, dv), BF16)

    def chunk(c):
        return pl.ds(c * tk, tk)

    for u in range(nsub):
        q = q_ref[0, :, u * tq:(u + 1) * tq, :].reshape(rows, dk)
        s_bufs = (s_a.at[u], s_b.at[u])
        m_u, acc_u = m_s.at[u], acc_s.at[u]

        def scores(c):
            return _dot_nt(q, k_ref[0, 0, chunk(c), :])

        def step(c):
            s_cur, s_nxt = s_bufs[c % 2], s_bufs[(c + 1) % 2]
            if c + 1 < nchunks:
                s_nxt[...] = scores(c + 1)
            cols = [s_cur[:, j * LANES:(j + 1) * LANES] for j in range(ncol)]
            mc = functools.reduce(jnp.maximum, cols)
            m_prev = m_u[...]
            m_new = jnp.maximum(m_prev, jnp.max(mc, axis=-1, keepdims=True))
            alpha = jnp.exp2(m_prev - m_new)
            p = jnp.concatenate([jnp.exp2((cj - m_new).astype(BF16)) for cj in cols], axis=-1)
            acc_u[...] = jnp.concatenate([alpha, alpha], axis=-1) * acc_u[...] + _dot(p, vx[chunk(c), :])
            m_u[...] = m_new

        m_u[...] = jnp.full(m_u.shape, NEG_BIG, F32)
        acc_u[...] = jnp.zeros(acc_u.shape, F32)
        s_bufs[0][...] = scores(0)
        for c in range(nchunks):
            step(c)
        o = acc_u[:, 0:dv] / acc_u[:, dv:2 * dv]
        for hh in range(nh):
            o_ref[0, u * tq:(u + 1) * tq, hh * dv:(hh + 1) * dv] = o[hh * tq:(hh + 1) * tq].astype(BF16)


def _cast_plan(w, nsteps):
    rows = w.shape[0]
    nb = nsteps
    while nb > 1 and (rows % nb or (rows // nb) % BF16_ROWS):
        nb //= 2
    return rows // nb, nsteps // nb


def _attention(q, k, v, tq, tk, nsub, casts=()):
    B, H, L, dk = q.shape
    G, S, dv = k.shape[1], k.shape[2], v.shape[3]
    nh = H // G
    tb = nsub * tq
    ni = L // tb
    plans = [_cast_plan(w, B * G * ni) for w in casts]

    def cast_spec(w, plan):
        rb, spb = plan
        return pl.BlockSpec((rb, w.shape[1]), lambda b, g, i: (((b * G + g) * ni + i) // spb, 0))

    cast_specs = [cast_spec(w, p) for w, p in zip(casts, plans)]
    return pl.pallas_call(
        functools.partial(_attn_kernel, tk=tk, nsub=nsub, ncast=len(casts)),
        grid=(B, G, ni),
        in_specs=[
            pl.BlockSpec((1, nh, tb, dk), lambda b, g, i: (b, g, i, 0)),
            pl.BlockSpec((1, 1, S, dk), lambda b, g, i: (b, g, 0, 0)),
            pl.BlockSpec((1, 1, S, dv), lambda b, g, i: (b, g, 0, 0)),
            *cast_specs,
        ],
        out_specs=(pl.BlockSpec((1, tb, nh * dv), lambda b, g, i: (b, i, g)), *cast_specs),
        out_shape=(jax.ShapeDtypeStruct((B, L, H * dv), BF16),
                   *[jax.ShapeDtypeStruct(w.shape, BF16) for w in casts]),
        scratch_shapes=[
            pltpu.VMEM((S, 2 * dv), BF16),
            pltpu.VMEM((nsub, nh * tq, tk), F32),
            pltpu.VMEM((nsub, nh * tq, tk), F32),
            pltpu.VMEM((nsub, nh * tq, LANES), F32),
            pltpu.VMEM((nsub, nh * tq, 2 * dv), F32),
        ],
        compiler_params=_params(("parallel", "parallel", "arbitrary")), name="attn",
    )(q, k, v, *casts)


def _mix_kernel(x_ref, g_ref, ya_ref, yb_ref, woa_ref, wob_ref, wout_ref, ncross_ref, wxq_ref, kvx_ref,
                wxo_ref, nffn_ref, x2_ref, hf_ref, x1_even, x1_odd):
    n = pl.program_id(0)
    D = x_ref.shape[2]
    kw = X_HEADS * X_HEAD_DIM

    @pl.when(n == 0)
    def _():
        x1_odd[...] = jnp.zeros_like(x1_odd)

    def body(x1_w, x1_r):
        a = _dot(ya_ref[0], woa_ref[...])
        x1 = x1_r[...]
        hc = _rms(x1, ncross_ref[...]).astype(BF16)
        q = _dot(hc, wxq_ref[...]) * (X_HEAD_DIM ** -0.5)
        b = _dot(yb_ref[0], wob_ref[...])
        outs = []
        for hh in range(X_HEADS):
            sl = slice(X_HEAD_DIM * hh, X_HEAD_DIM * (hh + 1))
            s = _dot_nt(q[:, sl].astype(BF16), kvx_ref[0, :, sl])
            p = jnp.exp(s - jnp.max(s, axis=-1, keepdims=True))
            l = jnp.sum(p, axis=-1, keepdims=True)
            o = _dot(p.astype(BF16), kvx_ref[0, :, kw + X_HEAD_DIM * hh:kw + X_HEAD_DIM * (hh + 1)]) / l
            outs.append(o.astype(BF16))
        x2 = x1 + _dot(jnp.concatenate(outs, axis=-1), wxo_ref[...])
        x2_ref[0] = x2
        hf_ref[0] = _rms(x2, nffn_ref[...]).astype(BF16)
        m = g_ref[:, 0:D].astype(F32) * a + g_ref[:, D:2 * D].astype(F32) * b
        x1_w[...] = x_ref[0] + _dot(m.astype(BF16), wout_ref[...])

    @pl.when(n % 2 == 0)
    def _():
        body(x1_even, x1_odd)

    @pl.when(n % 2 == 1)
    def _():
        body(x1_odd, x1_even)


def _mix(x, g2d, ya, yb, woa, wob, wout, ncross, wxq, kvx, wxo, nffn, tm):
    B, L, D = x.shape
    ni = L // tm
    nt = B * ni

    def cur(n):
        t = jnp.minimum(n, nt - 1)
        return t // ni, t % ni

    def prv(n):
        t = jnp.maximum(n - 1, 0)
        return t // ni, t % ni

    tok_cur = lambda n: (*cur(n), 0)
    tok_prv = lambda n: (*prv(n), 0)
    return pl.pallas_call(
        _mix_kernel, grid=(nt + 1,),
        in_specs=[
            pl.BlockSpec((1, tm, D), tok_cur),
            pl.BlockSpec((tm, 2 * D), lambda n: (jnp.minimum(n, nt - 1), 0)),
            pl.BlockSpec((1, tm, ya.shape[2]), tok_cur),
            pl.BlockSpec((1, tm, yb.shape[2]), tok_cur),
            _resident(woa.shape), _resident(wob.shape), _resident(wout.shape), _resident(ncross.shape),
            _resident(wxq.shape),
            pl.BlockSpec((1,) + kvx.shape[1:], lambda n: (prv(n)[0], 0, 0)),
            _resident(wxo.shape), _resident(nffn.shape),
        ],
        out_specs=(pl.BlockSpec((1, tm, D), tok_prv), pl.BlockSpec((1, tm, D), tok_prv)),
        out_shape=(jax.ShapeDtypeStruct((B, L, D), F32), jax.ShapeDtypeStruct((B, L, D), BF16)),
        scratch_shapes=[pltpu.VMEM((tm, D), F32), pltpu.VMEM((tm, D), F32)],
        compiler_params=_params(("arbitrary",)), name="mix",
    )(x, g2d, ya, yb, woa, wob, wout, ncross, wxq, kvx, wxo, nffn)


def _ffn_kernel(hm_ref, hp_ref, hn_ref, x2_ref, wa_ref, wb_ref, cwa_ref, cwb_ref, wd_ref, nfin_ref,
                o_ref, hbuf, ua, ub, *, tm):
    i, f = pl.program_id(1), pl.program_id(2)
    halo = BF16_ROWS

    @pl.when(f == 0)
    def _():
        prev, nxt = hp_ref[0], hn_ref[0]
        hbuf[0:halo] = jnp.where(i == 0, jnp.zeros_like(prev), prev)
        hbuf[halo:halo + tm] = hm_ref[0]
        hbuf[halo + tm:] = jnp.where(i == pl.num_programs(1) - 1, jnp.zeros_like(nxt), nxt)
        o_ref[...] = jnp.zeros_like(o_ref)

    hb = hbuf[...]

    def conv(u_ref, cw):
        return (u_ref[halo:halo + tm] * cw[1:2] + cw[3:4] + u_ref[halo - 1:halo - 1 + tm] * cw[0:1]
                + u_ref[halo + 1:halo + 1 + tm] * cw[2:3])

    ua[...] = _dot(hb, wa_ref[...])
    ub[...] = _dot(hb, wb_ref[...])
    a = conv(ua, cwa_ref[...])
    b = conv(ub, cwb_ref[...])
    act = a * (1.0 / (1.0 + jnp.exp(-a))) * b
    o_ref[0] += _dot(act.astype(BF16), wd_ref[...])

    @pl.when(f == pl.num_programs(2) - 1)
    def _():
        o_ref[0] = _rms(x2_ref[0] + o_ref[0], nfin_ref[...])


def _ffn(hf, x2, wup, cw, wdown, nfin, tm, tf):
    B, L, D = hf.shape
    dff = wdown.shape[0]
    nf = dff // tf
    hb = tm // BF16_ROWS
    last = L // BF16_ROWS - 1
    tok = lambda b, i, f: (b, i, 0)
    return pl.pallas_call(
        functools.partial(_ffn_kernel, tm=tm),
        grid=(B, L // tm, nf),
        in_specs=[
            pl.BlockSpec((1, tm, D), tok),
            pl.BlockSpec((1, BF16_ROWS, D), lambda b, i, f: (b, jnp.maximum(i * hb - 1, 0), 0)),
            pl.BlockSpec((1, BF16_ROWS, D), lambda b, i, f: (b, jnp.minimum((i + 1) * hb, last), 0)),
            pl.BlockSpec((1, tm, D), tok),
            pl.BlockSpec((D, tf), lambda b, i, f: (0, f)),
            pl.BlockSpec((D, tf), lambda b, i, f: (0, f + nf)),
            pl.BlockSpec((8, tf), lambda b, i, f: (0, f)),
            pl.BlockSpec((8, tf), lambda b, i, f: (0, f + nf)),
            pl.BlockSpec((tf, D), lambda b, i, f: (f, 0)),
            pl.BlockSpec(nfin.shape, lambda b, i, f: (0, 0)),
        ],
        out_specs=pl.BlockSpec((1, tm, D), tok),
        out_shape=jax.ShapeDtypeStruct((B, L, D), F32),
        scratch_shapes=[pltpu.VMEM((tm + 2 * BF16_ROWS, D), BF16),
                        pltpu.VMEM((tm + 2 * BF16_ROWS, tf), F32), pltpu.VMEM((tm + 2 * BF16_ROWS, tf), F32)],
        compiler_params=_params(("parallel", "parallel", "arbitrary")), name="ffn",
    )(hf, hf, hf, x2, wup, wup, cw, cw, wdown, nfin)


def _rope_tables(L):
    half = MLA_ROPE // 2
    inv = np.power(np.float32(ROPE_THETA), (-2.0 * np.arange(half, dtype=np.float32) / (2 * half)).astype(np.float32))
    t = np.arange(L)

    def cs(pos):
        ang = (pos.astype(np.float32)[:, None] * inv[None, :]).astype(np.float32)
        return np.cos(ang).astype(np.float32), np.sin(ang).astype(np.float32)

    z = np.zeros((L, half), np.float32)
    (ct, st), (cr, sr), (cc, sc) = cs(t), cs(t // GRID_W), cs(t % GRID_W)
    cat = lambda *p: np.concatenate(p, axis=-1)
    return jnp.asarray(np.stack([
        cat(ct, ct, z, z), cat(-st, z, z, z), cat(z, st, z, z),
        cat(cr, cr, cc, cc), cat(-sr, z, -sc, z), cat(z, sr, z, sc),
    ]))


def kernel(x, mem, norm_mix, w_in, mla_q_norm, w_uq, mla_kv_norm, w_ukv, gqa_q_norm, gqa_k_norm, w_o_mla,
           w_o_gqa, w_gate, b_gate, w_out, norm_cross, norm_mem, w_xq, w_xkv, w_xo, norm_ffn, w_up, conv_w,
           conv_b, w_down, norm_final):
    B, L, D = x.shape
    depth = w_in.shape[0]
    assert depth == 1, "the final norm is fused into the (single) layer's FFN kernel"
    tabs = _rope_tables(L)
    row = lambda v: v.reshape(1, -1).astype(F32)
    tm_prep = min(256, L)
    tm_gate = min(1024, L)
    tm_mix = min(256, L)
    tm_ffn = min(512, L)
    tq_a = min(512, L)
    tq_b = min(128, L)
    tk = min(512, L)
    nsub_a = 4 if L >= 8 * tq_a else 1
    nsub_b = 8 if L >= 8 * tq_b else 1

    for l in range(depth):
        wi = w_in[l]
        o_kr = MLA_Q_RANK + MLA_KV_RANK
        o_qb = o_kr + MLA_ROPE
        assert o_qb + (GQA_HEADS + 2 * GQA_KV_HEADS) * GQA_HEAD_DIM == wi.shape[1]
        wlat = wi[:, :o_kr].astype(BF16)
        wgqa = wi[:, o_qb:].astype(BF16)
        wkr = jnp.pad(wi[:, o_kr:o_qb], ((0, 0), (0, LANES - MLA_ROPE))).astype(BF16)
        wuq = jnp.pad(w_uq[l].reshape(MLA_Q_RANK, MLA_HEADS, MLA_NOPE + MLA_ROPE),
                      ((0, 0), (0, 0), (0, 2 * LANES - MLA_NOPE - MLA_ROPE))
                      ).reshape(MLA_Q_RANK, MLA_HEADS * 2 * LANES).astype(BF16)
        wkv = w_ukv[l].reshape(MLA_KV_RANK, MLA_HEADS, MLA_NOPE + MLA_V)
        wukv = jnp.concatenate([wkv[:, :, :MLA_NOPE].reshape(MLA_KV_RANK, -1),
                                wkv[:, :, MLA_NOPE:].reshape(MLA_KV_RANK, -1)], axis=1).astype(BF16)
        wx = w_xkv[l].reshape(D, X_HEADS, 2 * X_HEAD_DIM)
        wxkv = jnp.concatenate([wx[:, :, :X_HEAD_DIM].reshape(D, -1),
                                wx[:, :, X_HEAD_DIM:].reshape(D, -1)], axis=1).astype(BF16)
        cw = jnp.concatenate([conv_w[l], conv_b[l][None, :],
                              jnp.zeros((8 - CONV_W - 1, conv_w.shape[2]), F32)], axis=0)

        h, qm, km, vm, qg, kg, vg = _prep(
            x, row(norm_mix[l]), wlat, wgqa, wkr, row(mla_q_norm[l]), wuq, row(mla_kv_norm[l]), wukv,
            row(gqa_q_norm[l]), row(gqa_k_norm[l]), tabs, tm_prep)
        ya, wup, wg = _attention(qm, km, vm, tq_a, tk, nsub_a, casts=(w_up[l], w_gate[l]))
        yb, wdown, wout, woa, wob, wxq, wxo = _attention(
            qg, kg, vg, tq_b, tk, nsub_b,
            casts=(w_down[l], w_out[l], w_o_mla[l], w_o_gqa[l], w_xq[l], w_xo[l]))
        g = _gate(h.reshape(B * L, D), wg, row(b_gate[l]), tm_gate)
        kvx = _memkv(mem, row(norm_mem[l]), wxkv)
        x2, hf = _mix(x, g, ya, yb, woa, wob, wout, row(norm_cross[l]), wxq, kvx, wxo, row(norm_ffn[l]), tm_mix)
        x = _ffn(hf, x2, wup, cw, wdown, row(norm_final), tm_ffn, min(512, w_down.shape[1]))
    return x
```

```python
import functools

import jax
import jax.numpy as jnp
import numpy as np
from jax import lax
from jax.experimental import pallas as pl
from jax.experimental.pallas import tpu as pltpu

F32 = jnp.float32
BF16 = jnp.bfloat16

EPS = 1e-6
ROPE_THETA = 10000.0
GRID_W = 64
MLA_HEADS = 8
MLA_Q_RANK = 512
MLA_KV_RANK = 512
MLA_NOPE = 128
MLA_ROPE = 64
MLA_V = 128
GQA_HEADS = 8
GQA_KV_HEADS = 2
GQA_HEAD_DIM = 128
X_HEADS = 4
X_HEAD_DIM = 128
CONV_W = 3

LANES = 128
BF16_ROWS = 16
VMEM_LIMIT = 56 * 1024 * 1024
NEG_BIG = -1e30
LOG2E = float(np.log2(np.e))


def _params(semantics):
    return pltpu.CompilerParams(dimension_semantics=semantics, vmem_limit_bytes=VMEM_LIMIT)


def _resident(shape):
    zeros = (0,) * len(shape)
    return pl.BlockSpec(shape, lambda *_: zeros, pipeline_mode=pl.Buffered(1))


def _rms(x, gain):
    ms = jnp.mean(x * x, axis=-1, keepdims=True)
    return x * lax.rsqrt(ms + EPS) * gain


def _dot(a, b):
    return jnp.dot(a, b, preferred_element_type=F32)


def _dot_nt(a, b):
    return lax.dot_general(a, b, (((1,), (1,)), ((), ())), preferred_element_type=F32)


def _rope(t, c, sa, sb):
    return t * c + pltpu.roll(t, LANES - 32, 1) * sa + pltpu.roll(t, 32, 1) * sb


def _prep_kernel(*refs):
    *io, h_even, h_odd = refs
    n = pl.program_id(0)

    @pl.when(n == 0)
    def _():
        h_odd[...] = jnp.zeros_like(h_odd)

    @pl.when(n % 2 == 0)
    def _():
        _prep_body(h_even, h_odd, *io)

    @pl.when(n % 2 == 1)
    def _():
        _prep_body(h_odd, h_even, *io)


def _prep_body(h_w, h_r, x_ref, nmix_ref, wlat_ref, wgqa_ref, wkr_ref, qn_ref, wuq_ref, kvn_ref, wukv_ref,
               gq_ref, gk_ref, tab_ref, h_ref, qm_ref, km_ref, vm_ref, qg_ref, kg_ref, vg_ref):
    h = h_r[...]
    zl = _dot(h, wlat_ref[...])
    h_next = _rms(x_ref[0], nmix_ref[...]).astype(BF16)
    h_ref[0] = h_next
    h_w[...] = h_next
    zg = _dot(h, wgqa_ref[...])
    zr = _dot(h, wkr_ref[...])
    c1, sa1, sb1 = tab_ref[0], tab_ref[1], tab_ref[2]
    c2, sa2, sb2 = tab_ref[3], tab_ref[4], tab_ref[5]

    scale_a = (MLA_NOPE + MLA_ROPE) ** -0.5 * LOG2E
    cq = _rms(zl[:, 0:MLA_Q_RANK], qn_ref[...]).astype(BF16)
    q = _dot(cq, wuq_ref[...])
    for hh in range(MLA_HEADS):
        base = 2 * LANES * hh
        qm_ref[0, hh, :, 0:LANES] = (q[:, base:base + LANES] * scale_a).astype(BF16)
        r = _rope(q[:, base + LANES:base + 2 * LANES], c1, sa1, sb1)
        qm_ref[0, hh, :, LANES:2 * LANES] = (r * scale_a).astype(BF16)

    ckv = _rms(zl[:, MLA_Q_RANK:MLA_Q_RANK + MLA_KV_RANK], kvn_ref[...]).astype(BF16)
    kv = _dot(ckv, wukv_ref[...])
    kr = _rope(zr, c1, sa1, sb1).astype(BF16)
    for hh in range(MLA_HEADS):
        km_ref[0, hh, :, 0:LANES] = kv[:, LANES * hh:LANES * (hh + 1)].astype(BF16)
        km_ref[0, hh, :, LANES:2 * LANES] = kr
        vb = MLA_HEADS * MLA_NOPE + MLA_V * hh
        vm_ref[0, hh] = kv[:, vb:vb + MLA_V].astype(BF16)

    scale_b = GQA_HEAD_DIM ** -0.5 * LOG2E
    dh = GQA_HEAD_DIM
    o_k = GQA_HEADS * dh
    o_v = o_k + GQA_KV_HEADS * dh
    for hh in range(GQA_HEADS):
        t = _rms(zg[:, dh * hh:dh * (hh + 1)], gq_ref[...])
        qg_ref[0, hh] = (_rope(t, c2, sa2, sb2) * scale_b).astype(BF16)
    for hh in range(GQA_KV_HEADS):
        t = _rms(zg[:, o_k + dh * hh:o_k + dh * (hh + 1)], gk_ref[...])
        kg_ref[0, hh] = _rope(t, c2, sa2, sb2).astype(BF16)
        vg_ref[0, hh] = zg[:, o_v + dh * hh:o_v + dh * (hh + 1)].astype(BF16)


def _prep(x, nmix, wlat, wgqa, wkr, qn, wuq, kvn, wukv, gq, gk, tabs, tm):
    B, L, D = x.shape
    ni = L // tm
    nt = B * ni
    grid = (nt + 1,)

    def cur(n):
        t = jnp.minimum(n, nt - 1)
        return t // ni, t % ni

    def prv(n):
        t = jnp.maximum(n - 1, 0)
        return t // ni, t % ni

    tok = lambda n: (*cur(n), 0)
    head = lambda n: (prv(n)[0], 0, prv(n)[1], 0)
    out_shape = (
        jax.ShapeDtypeStruct((B, L, D), BF16),
        jax.ShapeDtypeStruct((B, MLA_HEADS, L, 2 * LANES), BF16),
        jax.ShapeDtypeStruct((B, MLA_HEADS, L, 2 * LANES), BF16),
        jax.ShapeDtypeStruct((B, MLA_HEADS, L, MLA_V), BF16),
        jax.ShapeDtypeStruct((B, GQA_HEADS, L, GQA_HEAD_DIM), BF16),
        jax.ShapeDtypeStruct((B, GQA_KV_HEADS, L, GQA_HEAD_DIM), BF16),
        jax.ShapeDtypeStruct((B, GQA_KV_HEADS, L, GQA_HEAD_DIM), BF16),
    )
    out_specs = (
        pl.BlockSpec((1, tm, D), tok),
        pl.BlockSpec((1, MLA_HEADS, tm, 2 * LANES), head),
        pl.BlockSpec((1, MLA_HEADS, tm, 2 * LANES), head),
        pl.BlockSpec((1, MLA_HEADS, tm, MLA_V), head),
        pl.BlockSpec((1, GQA_HEADS, tm, GQA_HEAD_DIM), head),
        pl.BlockSpec((1, GQA_KV_HEADS, tm, GQA_HEAD_DIM), head),
        pl.BlockSpec((1, GQA_KV_HEADS, tm, GQA_HEAD_DIM), head),
    )
    in_specs = [
        pl.BlockSpec((1, tm, D), tok),
        _resident(nmix.shape), _resident(wlat.shape), _resident(wgqa.shape), _resident(wkr.shape),
        _resident(qn.shape), _resident(wuq.shape),
        _resident(kvn.shape), _resident(wukv.shape), _resident(gq.shape), _resident(gk.shape),
        pl.BlockSpec((6, tm, LANES), lambda n: (0, prv(n)[1], 0)),
    ]
    return pl.pallas_call(
        _prep_kernel, grid=grid, in_specs=in_specs, out_specs=out_specs, out_shape=out_shape,
        scratch_shapes=[pltpu.VMEM((tm, D), BF16), pltpu.VMEM((tm, D), BF16)],
        compiler_params=_params(("arbitrary",)), name="prep",
    )(x, nmix, wlat, wgqa, wkr, qn, wuq, kvn, wukv, gq, gk, tabs)


def _gate_kernel(h_ref, w_ref, b_ref, g_ref, *, chunk):
    h = h_ref[...]
    for c in range(w_ref.shape[1] // chunk):
        sl = slice(c * chunk, (c + 1) * chunk)
        pre = _dot(h, w_ref[:, sl]) + b_ref[:, sl]
        g_ref[:, sl] = (1.0 / (1.0 + jnp.exp(-pre))).astype(BF16)


def _gate(h2d, wg, bg, tm):
    T, D = h2d.shape
    N = wg.shape[1]
    return pl.pallas_call(
        functools.partial(_gate_kernel, chunk=1024),
        grid=(T // tm,),
        in_specs=[pl.BlockSpec((tm, D), lambda i: (i, 0)), _resident(wg.shape), _resident(bg.shape)],
        out_specs=pl.BlockSpec((tm, N), lambda i: (i, 0)),
        out_shape=jax.ShapeDtypeStruct((T, N), BF16),
        compiler_params=_params(("parallel",)), name="gate",
    )(h2d, wg, bg)


def _memkv_kernel(mem_ref, nmem_ref, w_ref, o_ref):
    mn = _rms(mem_ref[0], nmem_ref[...]).astype(BF16)
    o_ref[0] = _dot(mn, w_ref[...]).astype(BF16)


def _memkv(mem, nmem, wxkv):
    B, M, D = mem.shape
    N = wxkv.shape[1]
    return pl.pallas_call(
        _memkv_kernel, grid=(B,),
        in_specs=[pl.BlockSpec((1, M, D), lambda b: (b, 0, 0)), _resident(nmem.shape), _resident(wxkv.shape)],
        out_specs=pl.BlockSpec((1, M, N), lambda b: (b, 0, 0)),
        out_shape=jax.ShapeDtypeStruct((B, M, N), BF16),
        compiler_params=_params(("parallel",)), name="memkv",
    )(mem, nmem, wxkv)


def _attn_kernel(q_ref, k_ref, v_ref, *rest, tk, nsub, ncast):
    w32, o_ref, w16 = rest[:ncast], rest[ncast], rest[ncast + 1:2 * ncast + 1]
    vx, s_a, s_b, m_s, acc_s = rest[2 * ncast + 1:]
    for src, dst in zip(w32, w16):
        dst[...] = src[...].astype(BF16)
    nh, dk = q_ref.shape[1], q_ref.shape[3]
    tq = q_ref.shape[2] // nsub
    S, dv = v_ref.shape[2], v_ref.shape[3]
    rows = nh * tq
    nchunks = S // tk
    ncol = tk // LANES
    assert dv == LANES, "alpha (lane-replicated) is applied to the accumulator without a broadcast"

    @pl.when(pl.program_id(2) == 0)
    def _():
        vx[:, 0:dv] = v_ref[0, 0]
        vx[:, dv:2 * dv] = jnp.ones((S, dv), BF16)

    def chunk(c):
        return pl.ds(c * tk, tk)

    for u in range(nsub):
        q = q_ref[0, :, u * tq:(u + 1) * tq, :].reshape(rows, dk)
        s_bufs = (s_a.at[u], s_b.at[u])
        m_u, acc_u = m_s.at[u], acc_s.at[u]

        def scores(c):
            return _dot_nt(q, k_ref[0, 0, chunk(c), :])

        def step(c):
            s_cur, s_nxt = s_bufs[c % 2], s_bufs[(c + 1) % 2]
            if c + 1 < nchunks:
                s_nxt[...] = scores(c + 1)
            cols = [s_cur[:, j * LANES:(j + 1) * LANES] for j in range(ncol)]
            mc = functools.reduce(jnp.maximum, cols)
            m_prev = m_u[...]
            m_new = jnp.maximum(m_prev, jnp.max(mc, axis=-1, keepdims=True))
            alpha = jnp.exp2(m_prev - m_new)
            p = jnp.concatenate([jnp.exp2((cj - m_new).astype(BF16)) for cj in cols], axis=-1)
            acc_u[...] = jnp.concatenate([alpha, alpha], axis=-1) * acc_u[...] + _dot(p, vx[chunk(c), :])
            m_u[...] = m_new

        m_u[...] = jnp.full(m_u.shape, NEG_BIG, F32)
        acc_u[...] = jnp.zeros(acc_u.shape, F32)
        s_bufs[0][...] = scores(0)
        for c in range(nchunks):
            step(c)
        o = acc_u[:, 0:dv] / acc_u[:, dv:2 * dv]
        for hh in range(nh):
            o_ref[0, u * tq:(u + 1) * tq, hh * dv:(hh + 1) * dv] = o[hh * tq:(hh + 1) * tq].astype(BF16)


def _cast_plan(w, nsteps):
    rows = w.shape[0]
    nb = nsteps
    while nb > 1 and (rows % nb or (rows // nb) % BF16_ROWS):
        nb //= 2
    return rows // nb, nsteps // nb


def _attention(q, k, v, tq, tk, nsub, casts=()):
    B, H, L, dk = q.shape
    G, S, dv = k.shape[1], k.shape[2], v.shape[3]
    nh = H // G
    tb = nsub * tq
    ni = L // tb
    plans = [_cast_plan(w, B * G * ni) for w in casts]

    def cast_spec(w, plan):
        rb, spb = plan
        return pl.BlockSpec((rb, w.shape[1]), lambda b, g, i: (((b * G + g) * ni + i) // spb, 0))

    cast_specs = [cast_spec(w, p) for w, p in zip(casts, plans)]
    return pl.pallas_call(
        functools.partial(_attn_kernel, tk=tk, nsub=nsub, ncast=len(casts)),
        grid=(B, G, ni),
        in_specs=[
            pl.BlockSpec((1, nh, tb, dk), lambda b, g, i: (b, g, i, 0)),
            pl.BlockSpec((1, 1, S, dk), lambda b, g, i: (b, g, 0, 0)),
            pl.BlockSpec((1, 1, S, dv), lambda b, g, i: (b, g, 0, 0)),
            *cast_specs,
        ],
        out_specs=(pl.BlockSpec((1, tb, nh * dv), lambda b, g, i: (b, i, g)), *cast_specs),
        out_shape=(jax.ShapeDtypeStruct((B, L, H * dv), BF16),
                   *[jax.ShapeDtypeStruct(w.shape, BF16) for w in casts]),
        scratch_shapes=[
            pltpu.VMEM((S, 2 * dv), BF16),
            pltpu.VMEM((nsub, nh * tq, tk), F32),
            pltpu.VMEM((nsub, nh * tq, tk), F32),
            pltpu.VMEM((nsub, nh * tq, LANES), F32),
            pltpu.VMEM((nsub, nh * tq, 2 * dv), F32),
        ],
        compiler_params=_params(("parallel", "parallel", "arbitrary")), name="attn",
    )(q, k, v, *casts)


def _mix_kernel(x_ref, g_ref, ya_ref, yb_ref, woa_ref, wob_ref, wout_ref, ncross_ref, wxq_ref, kvx_ref,
                wxo_ref, nffn_ref, x2_ref, hf_ref, x1_even, x1_odd):
    n = pl.program_id(0)
    D = x_ref.shape[2]
    kw = X_HEADS * X_HEAD_DIM

    @pl.when(n == 0)
    def _():
        x1_odd[...] = jnp.zeros_like(x1_odd)

    def body(x1_w, x1_r):
        a = _dot(ya_ref[0], woa_ref[...])
        x1 = x1_r[...]
        hc = _rms(x1, ncross_ref[...]).astype(BF16)
        q = _dot(hc, wxq_ref[...]) * (X_HEAD_DIM ** -0.5)
        b = _dot(yb_ref[0], wob_ref[...])
        outs = []
        for hh in range(X_HEADS):
            sl = slice(X_HEAD_DIM * hh, X_HEAD_DIM * (hh + 1))
            s = _dot_nt(q[:, sl].astype(BF16), kvx_ref[0, :, sl])
            p = jnp.exp(s - jnp.max(s, axis=-1, keepdims=True))
            l = jnp.sum(p, axis=-1, keepdims=True)
            o = _dot(p.astype(BF16), kvx_ref[0, :, kw + X_HEAD_DIM * hh:kw + X_HEAD_DIM * (hh + 1)]) / l
            outs.append(o.astype(BF16))
        x2 = x1 + _dot(jnp.concatenate(outs, axis=-1), wxo_ref[...])
        x2_ref[0] = x2
        hf_ref[0] = _rms(x2, nffn_ref[...]).astype(BF16)
        m = g_ref[:, 0:D].astype(F32) * a + g_ref[:, D:2 * D].astype(F32) * b
        x1_w[...] = x_ref[0] + _dot(m.astype(BF16), wout_ref[...])

    @pl.when(n % 2 == 0)
    def _():
        body(x1_even, x1_odd)

    @pl.when(n % 2 == 1)
    def _():
        body(x1_odd, x1_even)


def _mix(x, g2d, ya, yb, woa, wob, wout, ncross, wxq, kvx, wxo, nffn, tm):
    B, L, D = x.shape
    ni = L // tm
    nt = B * ni

    def cur(n):
        t = jnp.minimum(n, nt - 1)
        return t // ni, t % ni

    def prv(n):
        t = jnp.maximum(n - 1, 0)
        return t // ni, t % ni

    tok_cur = lambda n: (*cur(n), 0)
    tok_prv = lambda n: (*prv(n), 0)
    return pl.pallas_call(
        _mix_kernel, grid=(nt + 1,),
        in_specs=[
            pl.BlockSpec((1, tm, D), tok_cur),
            pl.BlockSpec((tm, 2 * D), lambda n: (jnp.minimum(n, nt - 1), 0)),
            pl.BlockSpec((1, tm, ya.shape[2]), tok_cur),
            pl.BlockSpec((1, tm, yb.shape[2]), tok_cur),
            _resident(woa.shape), _resident(wob.shape), _resident(wout.shape), _resident(ncross.shape),
            _resident(wxq.shape),
            pl.BlockSpec((1,) + kvx.shape[1:], lambda n: (prv(n)[0], 0, 0)),
            _resident(wxo.shape), _resident(nffn.shape),
        ],
        out_specs=(pl.BlockSpec((1, tm, D), tok_prv), pl.BlockSpec((1, tm, D), tok_prv)),
        out_shape=(jax.ShapeDtypeStruct((B, L, D), F32), jax.ShapeDtypeStruct((B, L, D), BF16)),
        scratch_shapes=[pltpu.VMEM((tm, D), F32), pltpu.VMEM((tm, D), F32)],
        compiler_params=_params(("arbitrary",)), name="mix",
    )(x, g2d, ya, yb, woa, wob, wout, ncross, wxq, kvx, wxo, nffn)


def _ffn_kernel(hm_ref, hp_ref, hn_ref, x2_ref, wa_ref, wb_ref, cwa_ref, cwb_ref, wd_ref, nfin_ref,
                o_ref, hbuf, ua, ub, *, tm):
    i, f = pl.program_id(1), pl.program_id(2)
    halo = BF16_ROWS

    @pl.when(f == 0)
    def _():
        prev, nxt = hp_ref[0], hn_ref[0]
        hbuf[0:halo] = jnp.where(i == 0, jnp.zeros_like(prev), prev)
        hbuf[halo:halo + tm] = hm_ref[0]
        hbuf[halo + tm:] = jnp.where(i == pl.num_programs(1) - 1, jnp.zeros_like(nxt), nxt)
        o_ref[...] = jnp.zeros_like(o_ref)

    hb = hbuf[...]

    def conv(u_ref, cw):
        return (u_ref[halo:halo + tm] * cw[1:2] + cw[3:4] + u_ref[halo - 1:halo - 1 + tm] * cw[0:1]
                + u_ref[halo + 1:halo + 1 + tm] * cw[2:3])

    ua[...] = _dot(hb, wa_ref[...])
    ub[...] = _dot(hb, wb_ref[...])
    a = conv(ua, cwa_ref[...])
    b = conv(ub, cwb_ref[...])
    act = a * (1.0 / (1.0 + jnp.exp(-a))) * b
    o_ref[0] += _dot(act.astype(BF16), wd_ref[...])

    @pl.when(f == pl.num_programs(2) - 1)
    def _():
        o_ref[0] = _rms(x2_ref[0] + o_ref[0], nfin_ref[...])


def _ffn(hf, x2, wup, cw, wdown, nfin, tm, tf):
    B, L, D = hf.shape
    dff = wdown.shape[0]
    nf = dff // tf
    hb = tm // BF16_ROWS
    last = L // BF16_ROWS - 1
    tok = lambda b, i, f: (b, i, 0)
    return pl.pallas_call(
        functools.partial(_ffn_kernel, tm=tm),
        grid=(B, L // tm, nf),
        in_specs=[
            pl.BlockSpec((1, tm, D), tok),
            pl.BlockSpec((1, BF16_ROWS, D), lambda b, i, f: (b, jnp.maximum(i * hb - 1, 0), 0)),
            pl.BlockSpec((1, BF16_ROWS, D), lambda b, i, f: (b, jnp.minimum((i + 1) * hb, last), 0)),
            pl.BlockSpec((1, tm, D), tok),
            pl.BlockSpec((D, tf), lambda b, i, f: (0, f)),
            pl.BlockSpec((D, tf), lambda b, i, f: (0, f + nf)),
            pl.BlockSpec((8, tf), lambda b, i, f: (0, f)),
            pl.BlockSpec((8, tf), lambda b, i, f: (0, f + nf)),
            pl.BlockSpec((tf, D), lambda b, i, f: (f, 0)),
            pl.BlockSpec(nfin.shape, lambda b, i, f: (0, 0)),
        ],
        out_specs=pl.BlockSpec((1, tm, D), tok),
        out_shape=jax.ShapeDtypeStruct((B, L, D), F32),
        scratch_shapes=[pltpu.VMEM((tm + 2 * BF16_ROWS, D), BF16),
                        pltpu.VMEM((tm + 2 * BF16_ROWS, tf), F32), pltpu.VMEM((tm + 2 * BF16_ROWS, tf), F32)],
        compiler_params=_params(("parallel", "parallel", "arbitrary")), name="ffn",
    )(hf, hf, hf, x2, wup, wup, cw, cw, wdown, nfin)


def _rope_tables(L):
    half = MLA_ROPE // 2
    inv = np.power(np.float32(ROPE_THETA), (-2.0 * np.arange(half, dtype=np.float32) / (2 * half)).astype(np.float32))
    t = np.arange(L)

    def cs(pos):
        ang = (pos.astype(np.float32)[:, None] * inv[None, :]).astype(np.float32)
        return np.cos(ang).astype(np.float32), np.sin(ang).astype(np.float32)

    z = np.zeros((L, half), np.float32)
    (ct, st), (cr, sr), (cc, sc) = cs(t), cs(t // GRID_W), cs(t % GRID_W)
    cat = lambda *p: np.concatenate(p, axis=-1)
    return jnp.asarray(np.stack([
        cat(ct, ct, z, z), cat(-st, z, z, z), cat(z, st, z, z),
        cat(cr, cr, cc, cc), cat(-sr, z, -sc, z), cat(z, sr, z, sc),
    ]))


def kernel(x, mem, norm_mix, w_in, mla_q_norm, w_uq, mla_kv_norm, w_ukv, gqa_q_norm, gqa_k_norm, w_o_mla,
           w_o_gqa, w_gate, b_gate, w_out, norm_cross, norm_mem, w_xq, w_xkv, w_xo, norm_ffn, w_up, conv_w,
           conv_b, w_down, norm_final):
    B, L, D = x.shape
    depth = w_in.shape[0]
    assert depth == 1, "the final norm is fused into the (single) layer's FFN kernel"
    tabs = _rope_tables(L)
    row = lambda v: v.reshape(1, -1).astype(F32)
    tm_prep = min(256, L)
    tm_gate = min(1024, L)
    tm_mix = min(256, L)
    tm_ffn = min(512, L)
    tq_a = min(512, L)
    tq_b = min(128, L)
    tk = min(512, L)
    nsub_a = 4 if L >= 8 * tq_a else 1
    nsub_b = 8 if L >= 8 * tq_b else 1

    for l in range(depth):
        wi = w_in[l]
        o_kr = MLA_Q_RANK + MLA_KV_RANK
        o_qb = o_kr + MLA_ROPE
        assert o_qb + (GQA_HEADS + 2 * GQA_KV_HEADS) * GQA_HEAD_DIM == wi.shape[1]
        wlat = wi[:, :o_kr].astype(BF16)
        wgqa = wi[:, o_qb:].astype(BF16)
        wkr = jnp.pad(wi[:, o_kr:o_qb], ((0, 0), (0, LANES - MLA_ROPE))).astype(BF16)
        wuq = jnp.pad(w_uq[l].reshape(MLA_Q_RANK, MLA_HEADS, MLA_NOPE + MLA_ROPE),
                      ((0, 0), (0, 0), (0, 2 * LANES - MLA_NOPE - MLA_ROPE))
                      ).reshape(MLA_Q_RANK, MLA_HEADS * 2 * LANES).astype(BF16)
        wkv = w_ukv[l].reshape(MLA_KV_RANK, MLA_HEADS, MLA_NOPE + MLA_V)
        wukv = jnp.concatenate([wkv[:, :, :MLA_NOPE].reshape(MLA_KV_RANK, -1),
                                wkv[:, :, MLA_NOPE:].reshape(MLA_KV_RANK, -1)], axis=1).astype(BF16)
        wx = w_xkv[l].reshape(D, X_HEADS, 2 * X_HEAD_DIM)
        wxkv = jnp.concatenate([wx[:, :, :X_HEAD_DIM].reshape(D, -1),
                                wx[:, :, X_HEAD_DIM:].reshape(D, -1)], axis=1).astype(BF16)
        cw = jnp.concatenate([conv_w[l], conv_b[l][None, :],
                              jnp.zeros((8 - CONV_W - 1, conv_w.shape[2]), F32)], axis=0)

        h, qm, km, vm, qg, kg, vg = _prep(
            x, row(norm_mix[l]), wlat, wgqa, wkr, row(mla_q_norm[l]), wuq, row(mla_kv_norm[l]), wukv,
            row(gqa_q_norm[l]), row(gqa_k_norm[l]), tabs, tm_prep)
        ya, wup, wg = _attention(qm, km, vm, tq_a, tk, nsub_a, casts=(w_up[l], w_gate[l]))
        yb, wdown, wout, woa, wob, wxq, wxo = _attention(
            qg, kg, vg, tq_b, tk, nsub_b,
            casts=(w_down[l], w_out[l], w_o_mla[l], w_o_gqa[l], w_xq[l], w_xo[l]))
        g = _gate(h.reshape(B * L, D), wg, row(b_gate[l]), tm_gate)
        kvx = _memkv(mem, row(norm_mem[l]), wxkv)
        x2, hf = _mix(x, g, ya, yb, woa, wob, wout, row(norm_cross[l]), wxq, kvx, wxo, row(norm_ffn[l]), tm_mix)
        x = _ffn(hf, x2, wup, cw, wdown, row(norm_final), tm_ffn, min(512, w_down.shape[1]))
    return x
```
